```python
import jax
import jax.numpy as jnp
from jax import lax
import numpy as np

D_MODEL = 1024
BATCH = 4
SEQ = 8192
DEPTH = 2

N_GROUPS = 4
GROUP_WIDTH = D_MODEL // N_GROUPS
MIX_WIDTH = N_GROUPS * GROUP_WIDTH
ROPE_THETA = 10000.0
NORM_EPS = 1e-6
NEG_INF = -1e30
FORCE_SCORE = 1e9
Q_BLOCK = 128

LRU_WIDTH = GROUP_WIDTH
LRU_HEADS = 4
LRU_HEAD_DIM = LRU_WIDTH // LRU_HEADS
CONV_WIDTH = 4
LRU_C = 8.0

NSA_HEADS = 4
NSA_HEAD_DIM = GROUP_WIDTH // NSA_HEADS
CMP_LEN = 32
CMP_STRIDE = 16
CMP_HIDDEN = 256
SEL_LEN = 64
SEL_TOPN = 16
WINDOW = 512

GLA_HEADS = 4
GLA_DV = GROUP_WIDTH // GLA_HEADS
GLA_DK = GLA_DV // 2
GLA_GATE_RANK = 16
GLA_TAU = 16.0
GLA_CHUNK = 64

MLA_HEADS = 4
MLA_V_DIM = GROUP_WIDTH // MLA_HEADS
MLA_NOPE_DIM = 64
MLA_ROPE_DIM = 32
MLA_QK_DIM = MLA_NOPE_DIM + MLA_ROPE_DIM
MLA_Q_RANK = 192
MLA_KV_RANK = 128

_FF_RAW = -(-8 * D_MODEL // 3)
D_FF = -(-_FF_RAW // 256) * 256

IN_SPLITS = (
    LRU_WIDTH, LRU_WIDTH,
    NSA_HEADS * NSA_HEAD_DIM, 6 * NSA_HEAD_DIM, NSA_HEADS * 3,
    GLA_HEADS * GLA_DK, GLA_HEADS * GLA_DK, GLA_HEADS * GLA_DV,
    GLA_GATE_RANK, GLA_HEADS * GLA_DV,
    MLA_Q_RANK, MLA_KV_RANK, MLA_ROPE_DIM,
)
IN_COLS = sum(IN_SPLITS)

kernel_name = 'hybrid_hymba_rglru_nsa_gla_mla'


def rmsnorm(x, gain):
    xf = x.astype(jnp.float32)
    y = xf * lax.rsqrt(jnp.mean(xf * xf, axis=-1, keepdims=True) + NORM_EPS)
    return (y * gain.astype(jnp.float32)).astype(x.dtype)


def rope(x, positions):
    d = x.shape[-1]
    inv_freq = ROPE_THETA ** (-jnp.arange(0, d, 2, dtype=jnp.float32) / d)
    ang = positions.astype(jnp.float32)[..., None] * inv_freq
    cos = jnp.cos(ang)[:, :, None, :]
    sin = jnp.sin(ang)[:, :, None, :]
    xf = x.astype(jnp.float32)
    x1, x2 = xf[..., : d // 2], xf[..., d // 2:]
    return jnp.concatenate([x1 * cos - x2 * sin, x2 * cos + x1 * sin], axis=-1).astype(x.dtype)


def masked_softmax(s, mask):
    s = jnp.where(mask, s, NEG_INF)
    m = jnp.max(s, axis=-1, keepdims=True)
    p = jnp.where(mask, jnp.exp(s - m), 0.0)
    return p / jnp.maximum(jnp.sum(p, axis=-1, keepdims=True), 1e-30)


def rglru_mixer(xa, gate, conv_w, conv_b, wa, ba, wx, bx, lam):
    dtype = xa.dtype
    B, S, W = xa.shape
    xf = xa.astype(jnp.float32)
    xc = lax.conv_general_dilated(
        xf, conv_w.astype(jnp.float32)[:, None, :], window_strides=(1,),
        padding=[(CONV_WIDTH - 1, 0)], dimension_numbers=('NWC', 'WIO', 'NWC'),
        feature_group_count=W) + conv_b.astype(jnp.float32)
    xh = xc.reshape(B, S, LRU_HEADS, LRU_HEAD_DIM)
    r = jax.nn.sigmoid(jnp.einsum('bshi,hij->bshj', xh, wa.astype(jnp.float32)).reshape(B, S, W) + ba)
    i = jax.nn.sigmoid(jnp.einsum('bshi,hij->bshj', xh, wx.astype(jnp.float32)).reshape(B, S, W) + bx)
    log_a = -LRU_C * r * jax.nn.softplus(-lam.astype(jnp.float32))
    a = jnp.exp(log_a)
    u = jnp.sqrt(-jnp.expm1(2.0 * log_a)) * (i * xc)

    def combine(left, right):
        a1, b1 = left
        a2, b2 = right
        return a1 * a2, a2 * b1 + b2

    _, h = lax.associative_scan(combine, (a, u), axis=1)
    y = h * jax.nn.gelu(gate.astype(jnp.float32))
    return y.astype(dtype)


def nsa_mixer(q, kv, gates, positions, cmp_pos, cmp_w1, cmp_b1, cmp_w2):
    dtype = q.dtype
    f32 = jnp.float32
    B, S, _ = q.shape
    H, Dh = NSA_HEADS, NSA_HEAD_DIM
    q = rope(q.reshape(B, S, H, Dh), positions).astype(f32)
    k_c, v_c, k_s, v_s, k_w, v_w = jnp.split(kv.astype(f32), 6, axis=-1)
    k_c, k_s, k_w = [rope(k[:, :, None, :], positions)[:, :, 0] for k in (k_c, k_s, k_w)]

    n_cmp = (S - CMP_LEN) // CMP_STRIDE + 1
    tok_idx = jnp.arange(n_cmp)[:, None] * CMP_STRIDE + jnp.arange(CMP_LEN)[None, :]

    def compress(t, j):
        blk = t[:, tok_idx] + cmp_pos[j].astype(f32)
        hid = jax.nn.gelu(blk.reshape(B, n_cmp, CMP_LEN * Dh) @ cmp_w1[j].astype(f32) + cmp_b1[j].astype(f32))
        return hid @ cmp_w2[j].astype(f32)

    k_cmp = compress(k_c, 0)
    v_cmp = compress(v_c, 1)
    cmp_start = jnp.arange(n_cmp) * CMP_STRIDE
    cmp_end = cmp_start + CMP_LEN - 1

    n_sel = S // SEL_LEN
    topn = min(SEL_TOPN, n_sel)
    sel_j = jnp.arange(n_sel)
    sel_start = sel_j * SEL_LEN
    overlap = ((cmp_start[:, None] < sel_start[None, :] + SEL_LEN)
               & (cmp_start[:, None] + CMP_LEN > sel_start[None, :])).astype(f32)
    k_sel_blk = k_s.reshape(B, n_sel, SEL_LEN, Dh)
    v_sel_blk = v_s.reshape(B, n_sel, SEL_LEN, Dh)

    k_win_pad = jnp.pad(k_w, ((0, 0), (WINDOW, 0), (0, 0)))
    v_win_pad = jnp.pad(v_w, ((0, 0), (WINDOW, 0), (0, 0)))

    g = jax.nn.sigmoid(gates.astype(f32)).reshape(B, S, H, 3)
    scale = Dh ** -0.5
    n_qb = S // Q_BLOCK
    q_blocks = (q * scale).reshape(B, n_qb, Q_BLOCK, H, Dh).transpose(1, 0, 2, 3, 4)
    g_blocks = g.reshape(B, n_qb, Q_BLOCK, H, 3).transpose(1, 0, 2, 3, 4)
    starts = jnp.arange(n_qb, dtype=jnp.int32) * Q_BLOCK

    def block(args):
        qb, gb, s0 = args
        t = s0 + jnp.arange(Q_BLOCK, dtype=jnp.int32)
        s_c = jnp.einsum('bqhd,bnd->bhqn', qb, k_cmp)
        p_c = masked_softmax(s_c, cmp_end[None, :] <= t[:, None])
        o_c = jnp.einsum('bhqn,bnd->bqhd', p_c, v_cmp)
        imp = jnp.einsum('bhqn,ns->bqs', p_c, overlap)
        cur = t // SEL_LEN
        valid = sel_start[None, :] <= t[:, None]
        forced = (sel_j[None, :] == 0) | (sel_j[None, :] == cur[:, None]) | (sel_j[None, :] == cur[:, None] - 1)
        score = jnp.where(forced, FORCE_SCORE, jnp.where(valid, imp, NEG_INF))
        _, idx = lax.top_k(score, topn)
        ks = jax.vmap(lambda kb, ix: kb[ix])(k_sel_blk, idx)
        vs = jax.vmap(lambda vb, ix: vb[ix])(v_sel_blk, idx)
        key_pos = idx[..., None] * SEL_LEN + jnp.arange(SEL_LEN)
        m_s = (key_pos <= t[None, :, None, None]).reshape(B, 1, Q_BLOCK, topn * SEL_LEN)
        s_s = jnp.einsum('bqhd,bqnld->bhqnl', qb, ks).reshape(B, H, Q_BLOCK, topn * SEL_LEN)
        p_s = masked_softmax(s_s, m_s)
        o_s = jnp.einsum('bhqm,bqmd->bqhd', p_s, vs.reshape(B, Q_BLOCK, topn * SEL_LEN, Dh))
        kw = lax.dynamic_slice_in_dim(k_win_pad, s0, WINDOW + Q_BLOCK, axis=1)
        vw = lax.dynamic_slice_in_dim(v_win_pad, s0, WINDOW + Q_BLOCK, axis=1)
        kpos = s0 - WINDOW + jnp.arange(WINDOW + Q_BLOCK, dtype=jnp.int32)
        m_w = (kpos[None, :] <= t[:, None]) & (kpos[None, :] > t[:, None] - WINDOW) & (kpos[None, :] >= 0)
        s_w = jnp.einsum('bqhd,bkd->bhqk', qb, kw)
        p_w = masked_softmax(s_w, m_w)
        o_w = jnp.einsum('bhqk,bkd->bqhd', p_w, vw)
        return gb[..., 0:1] * o_c + gb[..., 1:2] * o_s + gb[..., 2:3] * o_w

    out = lax.map(block, (q_blocks, g_blocks, starts))
    return out.transpose(1, 0, 2, 3, 4).reshape(B, S, H * Dh).astype(dtype)


def gla_mixer(q, k, v, g_lr, og, w_g2, b_g2, norm_g):
    dtype = q.dtype
    f32 = jnp.float32
    B, S, _ = q.shape
    H, C = GLA_HEADS, GLA_CHUNK
    n_c = S // C
    log_a = jax.nn.log_sigmoid(g_lr.astype(f32) @ w_g2.astype(f32) + b_g2.astype(f32)) / GLA_TAU

    def to_chunks(t, d):
        return t.astype(f32).reshape(B, n_c, C, H, d).transpose(1, 0, 3, 2, 4)

    qc = to_chunks(q, GLA_DK) * (GLA_DK ** -0.5)
    kc = to_chunks(k, GLA_DK)
    vc = to_chunks(v, GLA_DV)
    gc = to_chunks(log_a, GLA_DK)
    causal = jnp.tril(jnp.ones((C, C), dtype=bool))

    def step(state, xs):
        qi, ki, vi, gi = xs
        b = jnp.cumsum(gi, axis=2)
        o_inter = jnp.einsum('bhck,bhkv->bhcv', qi * jnp.exp(b), state)
        diff = b[:, :, :, None, :] - b[:, :, None, :, :]
        decay = jnp.exp(jnp.where(causal[:, :, None], diff, NEG_INF))
        attn = jnp.einsum('bhik,bhjk,bhijk->bhij', qi, ki, decay)
        o = o_inter + jnp.einsum('bhij,bhjv->bhiv', attn, vi)
        b_last = b[:, :, -1:, :]
        state = (jnp.exp(b_last[:, :, 0, :])[..., None] * state
                 + jnp.einsum('bhjk,bhjv->bhkv', ki * jnp.exp(b_last - b), vi))
        return state, o

    s_init = jnp.zeros((B, H, GLA_DK, GLA_DV), f32)
    _, o = lax.scan(step, s_init, (qc, kc, vc, gc))
    o = o.transpose(1, 0, 3, 2, 4).reshape(B, S, H, GLA_DV)
    o = rmsnorm(o, norm_g) * jax.nn.silu(og.astype(f32).reshape(B, S, H, GLA_DV))
    return o.reshape(B, S, H * GLA_DV).astype(dtype)


def mla_mixer(c_q, c_kv, k_rope, positions, q_norm, kv_norm, w_uq, w_ukv):
    dtype = c_q.dtype
    f32 = jnp.float32
    B, S, _ = c_q.shape
    H = MLA_HEADS
    q = (rmsnorm(c_q, q_norm) @ w_uq).reshape(B, S, H, MLA_QK_DIM)
    q = jnp.concatenate([q[..., :MLA_NOPE_DIM], rope(q[..., MLA_NOPE_DIM:], positions)], axis=-1)
    kv = (rmsnorm(c_kv, kv_norm) @ w_ukv).reshape(B, S, H, MLA_NOPE_DIM + MLA_V_DIM)
    kr = rope(k_rope[:, :, None, :], positions)
    k = jnp.concatenate([kv[..., :MLA_NOPE_DIM], jnp.broadcast_to(kr, (B, S, H, MLA_ROPE_DIM))], axis=-1).astype(f32)
    v = kv[..., MLA_NOPE_DIM:].astype(f32)
    n_qb = S // Q_BLOCK
    q_blocks = (q.astype(f32) * (MLA_QK_DIM ** -0.5)).reshape(B, n_qb, Q_BLOCK, H, MLA_QK_DIM).transpose(1, 0, 2, 3, 4)
    starts = jnp.arange(n_qb, dtype=jnp.int32) * Q_BLOCK
    key_idx = jnp.arange(S, dtype=jnp.int32)

    def block(args):
        qb, s0 = args
        t = s0 + jnp.arange(Q_BLOCK, dtype=jnp.int32)
        s = jnp.einsum('bqhd,bkhd->bhqk', qb, k)
        p = masked_softmax(s, key_idx[None, :] <= t[:, None])
        return jnp.einsum('bhqk,bkhd->bqhd', p, v)

    o = lax.map(block, (q_blocks, starts))
    return o.transpose(1, 0, 2, 3, 4).reshape(B, S, H * MLA_V_DIM).astype(dtype)


def setup_inputs(seed: int = 0) -> dict:
    key = jax.random.key(seed)
    ks = jax.random.split(key, 32)
    f32 = jnp.float32

    def nrm(k, shape, scale):
        return jax.random.normal(k, shape, f32) * scale

    def gain(k, shape):
        return 1.0 + 0.02 * jax.random.normal(k, shape, f32)

    u = jax.random.uniform(ks[10], (DEPTH, LRU_WIDTH), f32, minval=0.9, maxval=0.999)
    s = u ** (1.0 / LRU_C)
    lru_lambda = jnp.log(s) - jnp.log1p(-s)
    positions = (jnp.arange(SEQ, dtype=jnp.int32)[None, :]
                 + jax.random.randint(ks[1], (BATCH, 1), 0, SEQ, dtype=jnp.int32))
    return {
        'x': nrm(ks[0], (BATCH, SEQ, D_MODEL), 1.0),
        'positions': positions,
        'norm_mix': gain(ks[2], (DEPTH, D_MODEL)),
        'w_in': nrm(ks[3], (DEPTH, D_MODEL, IN_COLS), D_MODEL ** -0.5),
        'conv_w': nrm(ks[4], (DEPTH, CONV_WIDTH, LRU_WIDTH), CONV_WIDTH ** -0.5),
        'conv_b': nrm(ks[5], (DEPTH, LRU_WIDTH), 0.02),
        'lru_wa': nrm(ks[6], (DEPTH, LRU_HEADS, LRU_HEAD_DIM, LRU_HEAD_DIM), LRU_HEAD_DIM ** -0.5),
        'lru_ba': nrm(ks[7], (DEPTH, LRU_WIDTH), 0.02),
        'lru_wx': nrm(ks[8], (DEPTH, LRU_HEADS, LRU_HEAD_DIM, LRU_HEAD_DIM), LRU_HEAD_DIM ** -0.5),
        'lru_bx': nrm(ks[9], (DEPTH, LRU_WIDTH), 0.02),
        'lru_lambda': lru_lambda,
        'cmp_pos': nrm(ks[11], (DEPTH, 2, CMP_LEN, NSA_HEAD_DIM), 0.02),
        'cmp_w1': nrm(ks[12], (DEPTH, 2, CMP_LEN * NSA_HEAD_DIM, CMP_HIDDEN), (CMP_LEN * NSA_HEAD_DIM) ** -0.5),
        'cmp_b1': nrm(ks[13], (DEPTH, 2, CMP_HIDDEN), 0.02),
        'cmp_w2': nrm(ks[14], (DEPTH, 2, CMP_HIDDEN, NSA_HEAD_DIM), CMP_HIDDEN ** -0.5),
        'gla_wg2': nrm(ks[15], (DEPTH, GLA_GATE_RANK, GLA_HEADS * GLA_DK), GLA_GATE_RANK ** -0.5),
        'gla_bg2': nrm(ks[16], (DEPTH, GLA_HEADS * GLA_DK), 0.02),
        'gla_norm': gain(ks[17], (DEPTH, GLA_DV)),
        'mla_q_norm': gain(ks[18], (DEPTH, MLA_Q_RANK)),
        'mla_kv_norm': gain(ks[19], (DEPTH, MLA_KV_RANK)),
        'mla_w_uq': nrm(ks[20], (DEPTH, MLA_Q_RANK, MLA_HEADS * MLA_QK_DIM), MLA_Q_RANK ** -0.5),
        'mla_w_ukv': nrm(ks[21], (DEPTH, MLA_KV_RANK, MLA_HEADS * (MLA_NOPE_DIM + MLA_V_DIM)), MLA_KV_RANK ** -0.5),
        'group_norm': gain(ks[22], (DEPTH, MIX_WIDTH)),
        'w_out': nrm(ks[23], (DEPTH, MIX_WIDTH, D_MODEL), MIX_WIDTH ** -0.5),
        'norm_ffn': gain(ks[24], (DEPTH, D_MODEL)),
        'w_gate_up': nrm(ks[25], (DEPTH, D_MODEL, 2 * D_FF), D_MODEL ** -0.5),
        'w_down': nrm(ks[26], (DEPTH, D_FF, D_MODEL), D_FF ** -0.5),
        'final_norm': gain(ks[27], (D_MODEL,)),
    }


def reference(x, positions, norm_mix, w_in, conv_w, conv_b, lru_wa, lru_ba, lru_wx, lru_bx,
              lru_lambda, cmp_pos, cmp_w1, cmp_b1, cmp_w2, gla_wg2, gla_bg2, gla_norm,
              mla_q_norm, mla_kv_norm, mla_w_uq, mla_w_ukv, group_norm, w_out, norm_ffn,
              w_gate_up, w_down, final_norm):
    B, S, _ = x.shape
    offsets = [int(o) for o in np.cumsum(IN_SPLITS)[:-1]]
    for l in range(DEPTH):
        h = rmsnorm(x, norm_mix[l])
        (a_x, a_gate, b_q, b_kv, b_gate, c_q, c_k, c_v, c_glr, c_og,
         d_cq, d_ckv, d_kr) = jnp.split(h @ w_in[l], offsets, axis=-1)
        y_a = rglru_mixer(a_x, a_gate, conv_w[l], conv_b[l], lru_wa[l], lru_ba[l],
                          lru_wx[l], lru_bx[l], lru_lambda[l])
        y_b = nsa_mixer(b_q, b_kv, b_gate, positions, cmp_pos[l], cmp_w1[l], cmp_b1[l], cmp_w2[l])
        y_c = gla_mixer(c_q, c_k, c_v, c_glr, c_og, gla_wg2[l], gla_bg2[l], gla_norm[l])
        y_d = mla_mixer(d_cq, d_ckv, d_kr, positions, mla_q_norm[l], mla_kv_norm[l],
                        mla_w_uq[l], mla_w_ukv[l])
        y = jnp.stack([y_a, y_b, y_c, y_d], axis=2)
        y = rmsnorm(y, group_norm[l].reshape(N_GROUPS, GROUP_WIDTH)).reshape(B, S, MIX_WIDTH)
        x = x + y @ w_out[l]
        h = rmsnorm(x, norm_ffn[l])
        gate, up = jnp.split(h @ w_gate_up[l], 2, axis=-1)
        x = x + (jax.nn.silu(gate) * up) @ w_down[l]
    return rmsnorm(x, final_norm)
```

```python
import functools

import numpy as np
import jax
import jax.numpy as jnp
from jax import lax
from jax.experimental import pallas as pl
from jax.experimental.pallas import tpu as pltpu

F32 = jnp.float32
BF16 = jnp.bfloat16
HIGHEST = lax.Precision.HIGHEST

D_MODEL = 1024
GROUP_WIDTH = 256
ROPE_THETA = 10000.0
NORM_EPS = 1e-6
NEG_INF = -1e30
FORCE_SCORE = 1e9

LRU_C = 8.0
CONV_WIDTH = 4

NSA_HEADS = 4
NSA_HEAD_DIM = 64
CMP_LEN = 32
CMP_STRIDE = 16
CMP_HIDDEN = 256
SEL_LEN = 64
SEL_TOPN = 16
WINDOW = 512
NSA_TQ = 128
NSA_TK = 512
MAX_SEL_BLOCKS = 128

GLA_HEADS = 4
GLA_DV = 64
GLA_DK = 32
GLA_GATE_RANK = 16
GLA_TAU = 16.0
GLA_TILE = 128
GLA_SUB = 16

MLA_HEADS = 4
MLA_V_DIM = 64
MLA_NOPE_DIM = 64
MLA_ROPE_DIM = 32
MLA_QK_DIM = 96
MLA_Q_RANK = 192
MLA_KV_RANK = 128
MLA_TQ = 256
MLA_TK = 512

D_FF = 2816
FF_CHUNK = 256

LANES = 128
VMEM_LIMIT = 56 * 1024 * 1024

_OFF = {}
_o = 0
for _n, _w in (("a_x", 256), ("a_gate", 256), ("b_q", 256), ("k_c", 64), ("v_c", 64), ("k_s", 64), ("v_s", 64),
               ("k_w", 64), ("v_w", 64), ("b_gate", 12), ("c_q", 128), ("c_k", 128), ("c_v", 256), ("c_glr", 16),
               ("c_og", 256), ("d_cq", 192), ("d_ckv", 128), ("d_kr", 32)):
    _OFF[_n] = (_o, _w)
    _o += _w
IN_COLS = _o
_LAYOUT_A = (("a_x", 256), ("a_gate", 256))
_LAYOUT_B = (("b_q", 256), ("k_s", 64), ("k_c", 64), ("k_w", 64), (None, 64),
             ("v_c", 64), ("v_s", 64), ("v_w", 64), ("b_gate", 12), (None, 52))
_LAYOUT_C = (("c_q", 128), ("c_k", 128), ("c_v", 256), ("c_og", 256), ("c_glr", 16), (None, 112))
_LAYOUT_D = (("d_cq", 192), (None, 64), ("d_ckv", 128), (None, 64), ("d_kr", 32), (None, 32))
WA, WB, WC, WD = 512, 768, 896, 512


def _cparams(sem):
    return pltpu.CompilerParams(dimension_semantics=sem, vmem_limit_bytes=VMEM_LIMIT)


def _dot(a, b):
    return jnp.dot(a, b, preferred_element_type=F32)


def _dot_nt(a, b):
    return lax.dot_general(a, b, (((1,), (1,)), ((), ())), preferred_element_type=F32)


def _dot_tn(a, b):
    return lax.dot_general(a, b, (((0,), (0,)), ((), ())), preferred_element_type=F32)


def _dot_exact(a, b):
    return jnp.dot(a, b, precision=HIGHEST, preferred_element_type=F32)


def _rms(x, width):
    return x * lax.rsqrt(jnp.sum(x * x, axis=-1, keepdims=True) * (1.0 / width) + NORM_EPS)


def _inproj_kernel(x_ref, g_ref, w_ref, oa_ref, ob_ref, oc_ref, od_ref):
    h = (_rms(x_ref[...], D_MODEL) * g_ref[...]).astype(BF16)
    off = 0
    for o_ref in (oa_ref, ob_ref, oc_ref, od_ref):
        w = o_ref.shape[1]
        o_ref[...] = _dot(h, w_ref[:, off:off + w])
        off += w


def _inproj(x2, gain, w_p, tm=512):
    T = x2.shape[0]
    ntot = w_p.shape[1]
    return pl.pallas_call(
        _inproj_kernel,
        grid=(T // tm,),
        in_specs=[pl.BlockSpec((tm, D_MODEL), lambda i: (i, 0)),
                  pl.BlockSpec((1, D_MODEL), lambda i: (0, 0)),
                  pl.BlockSpec((D_MODEL, ntot), lambda i: (0, 0))],
        out_specs=[pl.BlockSpec((tm, w), lambda i: (i, 0)) for w in (WA, WB, WC, WD)],
        out_shape=[jax.ShapeDtypeStruct((T, w), F32) for w in (WA, WB, WC, WD)],
        compiler_params=_cparams(("parallel",)),
        name="inproj",
    )(x2, gain.reshape(1, D_MODEL), w_p)


def _rglru_kernel(pa_ref, cw_ref, cb_ref, wg_ref, bg_ref, lam_ref, o_ref, xbuf, a_s, u_s, h_s, hlast):
    ts = pa_ref.shape[1]
    W = GROUP_WIDTH

    @pl.when(pl.program_id(1) == 0)
    def _():
        xbuf[0:8, :] = jnp.zeros((8, W), F32)
        hlast[...] = jnp.zeros_like(hlast)

    xbuf[8:8 + ts, :] = pa_ref[0, :, 0:W]
    xc = cb_ref[...]
    for k in range(CONV_WIDTH):
        lo = 8 - (CONV_WIDTH - 1) + k
        xc = xc + cw_ref[k:k + 1, :] * xbuf[lo:lo + ts, :]
    xbuf[0:8, :] = xbuf[ts:ts + 8, :]
    gates = _dot(xc.astype(BF16), wg_ref[...]) + bg_ref[...]
    r = jax.nn.sigmoid(gates[:, 0:W])
    i = jax.nn.sigmoid(gates[:, W:2 * W])
    log_a = (-LRU_C) * r * jax.nn.softplus(-lam_ref[...])
    a = jnp.exp(log_a)
    u = jnp.sqrt(-jnp.tanh(log_a) * (a * a + 1.0)) * (i * xc)
    a_s[...] = a
    u_s[...] = u

    def body(t, h):
        h = a_s[pl.ds(t, 1), :] * h + u_s[pl.ds(t, 1), :]
        h_s[pl.ds(t, 1), :] = h
        return h

    hlast[...] = lax.fori_loop(0, ts, body, hlast[...], unroll=8)
    o_ref[0] = h_s[...] * jax.nn.gelu(pa_ref[0, :, W:2 * W])


def _rglru(pa, conv_w, conv_b, wg, bg, lam, ts=512):
    B, S, _ = pa.shape
    W = GROUP_WIDTH
    full = lambda shape: pl.BlockSpec(shape, lambda b, j: (0,) * len(shape))
    return pl.pallas_call(
        _rglru_kernel,
        grid=(B, S // ts),
        in_specs=[pl.BlockSpec((1, ts, WA), lambda b, j: (b, j, 0)),
                  full((CONV_WIDTH, W)), full((1, W)), full((W, 2 * W)), full((1, 2 * W)), full((1, W))],
        out_specs=pl.BlockSpec((1, ts, W), lambda b, j: (b, j, 0)),
        out_shape=jax.ShapeDtypeStruct((B, S, W), F32),
        scratch_shapes=[pltpu.VMEM((ts + 8, W), F32), pltpu.VMEM((ts, W), F32), pltpu.VMEM((ts, W), F32),
                        pltpu.VMEM((ts, W), F32), pltpu.VMEM((1, W), F32)],
        compiler_params=_cparams(("parallel", "arbitrary")),
        name="rglru",
    )(pa, conv_w, conv_b.reshape(1, W), wg, bg, lam.reshape(1, W))


def _gla_kernel(pc_ref, wg2_ref, bg2_ref, gn_ref, lcum_ref, lsum_ref, hsum_ref, gmean_ref, smask_ref, o_ref, st_ref):
    TT, SUB = GLA_TILE, GLA_SUB
    NB = TT // SUB
    KW = GLA_HEADS * GLA_DK
    VW = GLA_HEADS * GLA_DV

    @pl.when(pl.program_id(1) == 0)
    def _():
        st_ref[...] = jnp.zeros_like(st_ref)

    q = pc_ref[0, :, 0:KW] * (GLA_DK ** -0.5)
    k = pc_ref[0, :, KW:2 * KW]
    v = pc_ref[0, :, 2 * KW:2 * KW + VW]
    og = pc_ref[0, :, 2 * KW + VW:2 * KW + 2 * VW]
    glr = pc_ref[0, :, 2 * KW + 2 * VW:2 * KW + 2 * VW + LANES]
    log_a = jax.nn.log_sigmoid(_dot_exact(glr, wg2_ref[...]) + bg2_ref[...]) * (1.0 / GLA_TAU)
    b = _dot_exact(lcum_ref[...], log_a)
    bl = _dot_exact(lsum_ref[...], log_a)

    q3 = q.reshape(NB, SUB, KW)
    k3 = k.reshape(NB, SUB, KW)
    b3 = b.reshape(NB, SUB, KW)
    v3 = v.reshape(NB, SUB, VW)
    row = lax.broadcasted_iota(jnp.int32, (NB, SUB, KW), 1)
    terms = []
    for j in range(SUB):
        kj = jnp.broadcast_to(k3[:, j:j + 1, :], (NB, SUB, KW))
        bj = jnp.broadcast_to(b3[:, j:j + 1, :], (NB, SUB, KW))
        e = q3 * kj * jnp.exp(jnp.where(row >= j, b3 - bj, NEG_INF))
        terms.append(e.reshape(TT, KW))
    e_all = jnp.concatenate(terms, axis=0)
    e_hi = e_all.astype(BF16)
    e_lo = (e_all - e_hi.astype(F32)).astype(BF16)
    attn = _dot(e_hi, hsum_ref[...]) + _dot(e_lo, hsum_ref[...])
    o = jnp.zeros((NB, SUB, VW), F32)
    for j in range(SUB):
        vj = jnp.broadcast_to(v3[:, j:j + 1, :], (NB, SUB, VW))
        o = o + attn[j * TT:(j + 1) * TT, :].reshape(NB, SUB, VW) * vj
    o = o.reshape(TT, VW)

    qd = (q * jnp.exp(b)).astype(BF16)
    kd = (k * jnp.exp(bl - b)).astype(BF16)
    dec = jnp.exp(bl)
    vb = v.astype(BF16)
    st = st_ref[...]
    inter = []
    for n in range(NB):
        rows = slice(n * SUB, (n + 1) * SUB)
        inter.append(_dot_nt(qd[rows], st.astype(BF16)))
        st = st * dec[n * SUB:n * SUB + 1, :] + _dot_tn(vb[rows], kd[rows]) * smask_ref[...]
    st_ref[...] = st
    o = o + jnp.concatenate(inter, axis=0)

    ms = _dot_exact(o * o, gmean_ref[...])
    o_ref[0] = o * lax.rsqrt(ms + NORM_EPS) * gn_ref[...] * jax.nn.silu(og)


def _gla_consts():
    TT, SUB = GLA_TILE, GLA_SUB
    KW, VW = GLA_HEADS * GLA_DK, GLA_HEADS * GLA_DV
    i = np.arange(TT)
    same = (i[:, None] // SUB) == (i[None, :] // SUB)
    lcum = (same & (i[None, :] <= i[:, None])).astype(np.float32)
    lsum = same.astype(np.float32)
    hk = np.arange(KW) // GLA_DK
    hv = np.arange(VW) // GLA_DV
    hsum = (hk[:, None] == hv[None, :]).astype(np.float32)
    gmean = (hv[:, None] == hv[None, :]).astype(np.float32) / GLA_DV
    smask = (hv[:, None] == hk[None, :]).astype(np.float32)
    return (jnp.asarray(lcum), jnp.asarray(lsum), jnp.asarray(hsum, dtype=BF16), jnp.asarray(gmean),
            jnp.asarray(smask))


def _gla(pc, wg2_p, bg2, gn_t):
    B, S, _ = pc.shape
    TT = GLA_TILE
    KW, VW = GLA_HEADS * GLA_DK, GLA_HEADS * GLA_DV
    lcum, lsum, hsum, gmean, smask = _gla_consts()
    full = lambda shape: pl.BlockSpec(shape, lambda b, j: (0,) * len(shape))
    return pl.pallas_call(
        _gla_kernel,
        grid=(B, S // TT),
        in_specs=[pl.BlockSpec((1, TT, WC), lambda b, j: (b, j, 0)),
                  full((LANES, KW)), full((1, KW)), full((1, VW)), full((TT, TT)), full((TT, TT)),
                  full((KW, VW)), full((VW, VW)), full((VW, KW))],
        out_specs=pl.BlockSpec((1, TT, VW), lambda b, j: (b, j, 0)),
        out_shape=jax.ShapeDtypeStruct((B, S, VW), F32),
        scratch_shapes=[pltpu.VMEM((VW, KW), F32)],
        compiler_params=_cparams(("parallel", "arbitrary")),
        name="gla",
    )(pc, wg2_p, bg2.reshape(1, KW), gn_t.reshape(1, VW), lcum, lsum, hsum, gmean, smask)


def _rope_lanes(x, pos_f, invf_row, half):
    outs = []
    for c in range(x.shape[1] // LANES):
        xs = x[:, c * LANES:(c + 1) * LANES]
        ang = pos_f * invf_row[:, c * LANES:(c + 1) * LANES]
        lane = lax.broadcasted_iota(jnp.int32, xs.shape, 1)
        lo = (lane % (2 * half)) < half
        rot = jnp.where(lo, -pltpu.roll(xs, LANES - half, 1), pltpu.roll(xs, half, 1))
        outs.append(xs * jnp.cos(ang) + rot * jnp.sin(ang))
    return outs[0] if len(outs) == 1 else jnp.concatenate(outs, axis=1)


def _inv_freq_row(d, reps, lead_zeros=0, tail_zeros=0):
    inv = ROPE_THETA ** (-np.arange(0, d, 2, dtype=np.float32) / d)
    row = np.concatenate([np.zeros(lead_zeros, np.float32), np.tile(np.concatenate([inv, inv]), reps),
                          np.zeros(tail_zeros, np.float32)]).astype(np.float32)
    return jnp.asarray(row).reshape(1, -1)


def _mla_proj_kernel(pd_ref, pos_ref, qn_ref, kvn_ref, wq_ref, wkv_ref, invf_ref, q_ref, k_ref, v_ref):
    H = MLA_HEADS
    pos_f = pos_ref[0].astype(F32)
    cq = (_rms(pd_ref[0, :, 0:256], MLA_Q_RANK) * qn_ref[...]).astype(BF16)
    ckv = (_rms(pd_ref[0, :, 256:384], MLA_KV_RANK) * kvn_ref[...]).astype(BF16)
    kr = _rope_lanes(pd_ref[0, :, 384:512], pos_f, invf_ref[...], MLA_ROPE_DIM // 2)
    q_all = _dot(cq, wq_ref[...])
    kv_all = _dot(ckv, wkv_ref[...])
    for h in range(H):
        qh = _rope_lanes(q_all[:, h * LANES:(h + 1) * LANES], pos_f, invf_ref[...], MLA_ROPE_DIM // 2)
        q_ref[0, h] = (qh * (MLA_QK_DIM ** -0.5)).astype(BF16)
        k_ref[0, h] = (kv_all[:, h * LANES:(h + 1) * LANES] + kr).astype(BF16)
        v_ref[0, h] = kv_all[:, (H + h) * LANES:(H + h + 1) * LANES].astype(BF16)


def _mla_proj(pd, pos3, qn_p, kvn, wq_p, wkv_p, tm=512):
    B, S, _ = pd.shape
    H = MLA_HEADS
    invf = _inv_freq_row(MLA_ROPE_DIM, 1, lead_zeros=64, tail_zeros=32)
    full = lambda shape: pl.BlockSpec(shape, lambda b, j: (0,) * len(shape))
    hspec = pl.BlockSpec((1, H, tm, LANES), lambda b, j: (b, 0, j, 0))
    hshape = jax.ShapeDtypeStruct((B, H, S, LANES), BF16)
    return pl.pallas_call(
        _mla_proj_kernel,
        grid=(B, S // tm),
        in_specs=[pl.BlockSpec((1, tm, WD), lambda b, j: (b, j, 0)),
                  pl.BlockSpec((1, tm, 1), lambda b, j: (b, j, 0)),
                  full((1, 256)), full((1, LANES)), full((256, H * LANES)), full((LANES, 2 * H * LANES)),
                  full((1, LANES))],
        out_specs=[hspec, hspec, hspec],
        out_shape=[hshape, hshape, hshape],
        compiler_params=_cparams(("parallel", "parallel")),
        name="mla_proj",
    )(pd, pos3, qn_p, kvn.reshape(1, LANES), wq_p, wkv_p, invf)


def _softmax_step(q, k, v, m_ref, l_ref, acc_ref, mask):
    s = _dot_nt(q, k)
    if mask is not None:
        s = jnp.where(mask, s, NEG_INF)
    m_prev = m_ref[...]
    m_new = jnp.maximum(m_prev, jnp.max(s, axis=-1, keepdims=True))
    alpha = jnp.exp(m_prev - m_new)
    p = jnp.exp(s - m_new)
    l_ref[...] = alpha * l_ref[...] + jnp.sum(p, axis=-1, keepdims=True)
    acc_ref[...] = alpha * acc_ref[...] + _dot(p.astype(BF16), v)
    m_ref[...] = m_new


def _mla_attn_kernel(q_ref, k_ref, v_ref, o_ref, m_ref, l_ref, acc_ref):
    tq, tk = MLA_TQ, MLA_TK
    i = pl.program_id(1)
    s0 = i * tq
    n_full = s0 // tk
    outs = []
    for h in range(MLA_HEADS):
        q = q_ref[0, h]
        m_ref[...] = jnp.full(m_ref.shape, NEG_INF, F32)
        l_ref[...] = jnp.zeros(l_ref.shape, F32)
        acc_ref[...] = jnp.zeros(acc_ref.shape, F32)

        def body(j, carry):
            off = pl.multiple_of(j * tk, tk)
            _softmax_step(q, k_ref[0, h, pl.ds(off, tk), :], v_ref[0, h, pl.ds(off, tk), :],
                          m_ref, l_ref, acc_ref, None)
            return carry

        lax.fori_loop(0, n_full, body, 0)
        off = pl.multiple_of(n_full * tk, tk)
        t = s0 + lax.broadcasted_iota(jnp.int32, (tq, tk), 0)
        kpos = off + lax.broadcasted_iota(jnp.int32, (tq, tk), 1)
        _softmax_step(q, k_ref[0, h, pl.ds(off, tk), :], v_ref[0, h, pl.ds(off, tk), :],
                      m_ref, l_ref, acc_ref, kpos <= t)
        outs.append((acc_ref[...] / jnp.maximum(l_ref[...], 1e-30))[:, 0:MLA_V_DIM])
    o_ref[0] = jnp.concatenate(outs, axis=1)


def _mla_attn(q, k, v):
    B, H, S, _ = q.shape
    tq = MLA_TQ
    return pl.pallas_call(
        _mla_attn_kernel,
        grid=(B, S // tq),
        in_specs=[pl.BlockSpec((1, H, tq, LANES), lambda b, i: (b, 0, i, 0)),
                  pl.BlockSpec((1, H, S, LANES), lambda b, i: (b, 0, 0, 0)),
                  pl.BlockSpec((1, H, S, LANES), lambda b, i: (b, 0, 0, 0))],
        out_specs=pl.BlockSpec((1, tq, H * MLA_V_DIM), lambda b, i: (b, i, 0)),
        out_shape=jax.ShapeDtypeStruct((B, S, H * MLA_V_DIM), F32),
        scratch_shapes=[pltpu.VMEM((tq, 1), F32), pltpu.VMEM((tq, 1), F32), pltpu.VMEM((tq, LANES), F32)],
        compiler_params=_cparams(("parallel", "arbitrary")),
        name="mla_attn",
    )(q, k, v)


def _nsa_proj_kernel(pb_ref, pos_ref, invf_ref, q_ref, ksa_ref, vs_ref, kw_ref, vw_ref, kcvc_ref):
    tm = pb_ref.shape[1]
    H = NSA_HEADS
    pos = pos_ref[0]
    r = _rope_lanes(pb_ref[0, :, 0:512], pos.astype(F32), invf_ref[...], NSA_HEAD_DIM // 2)
    nr0 = pb_ref[0, :, 512:640]
    nr1 = pb_ref[0, :, 640:768]
    lane = lax.broadcasted_iota(jnp.int32, (tm, LANES), 1)
    low = lane < NSA_HEAD_DIM
    scale = NSA_HEAD_DIM ** -0.5
    for h in range(H):
        seg = r[:, (h // 2) * LANES:(h // 2 + 1) * LANES]
        if h % 2:
            seg = pltpu.roll(seg, NSA_HEAD_DIM, 1)
        q_ref[0, h] = jnp.where(low, seg * scale, 0.0).astype(BF16)
    kseg = r[:, 256:384]
    ksa_ref[0, :, 0:LANES] = jnp.where(low, kseg, 0.0).astype(BF16)
    blk = lax.shift_right_logical(lax.broadcasted_iota(jnp.int32, (tm, LANES), 0) + pl.program_id(1) * tm, 6)
    ksa_ref[0, :, LANES:2 * LANES] = jnp.where(blk == lane, 1.0, 0.0).astype(BF16)
    kw_ref[0] = r[:, 384:512].astype(BF16)
    vs_ref[0] = jnp.where(low, pltpu.roll(nr0, NSA_HEAD_DIM, 1), 0.0).astype(BF16)
    vw_ref[0] = jnp.where(low, nr1, 0.0).astype(BF16)
    kcvc_ref[0] = jnp.where(low, pltpu.roll(kseg, NSA_HEAD_DIM, 1), pltpu.roll(nr0, NSA_HEAD_DIM, 1))


def _nsa_proj(pb, pos3, tm=512):
    B, S, _ = pb.shape
    H = NSA_HEADS
    invf = _inv_freq_row(NSA_HEAD_DIM, 8)
    tok = lambda w: pl.BlockSpec((1, tm, w), lambda b, j: (b, j, 0))
    return pl.pallas_call(
        _nsa_proj_kernel,
        grid=(B, S // tm),
        in_specs=[tok(WB), tok(1), pl.BlockSpec((1, 512), lambda b, j: (0, 0))],
        out_specs=[pl.BlockSpec((1, H, tm, LANES), lambda b, j: (b, 0, j, 0)),
                   tok(2 * LANES), tok(LANES), tok(LANES), tok(LANES), tok(LANES)],
        out_shape=[jax.ShapeDtypeStruct((B, H, S, LANES), BF16),
                   jax.ShapeDtypeStruct((B, S, 2 * LANES), BF16),
                   jax.ShapeDtypeStruct((B, S, LANES), BF16),
                   jax.ShapeDtypeStruct((B, S, LANES), BF16),
                   jax.ShapeDtypeStruct((B, S, LANES), BF16),
                   jax.ShapeDtypeStruct((B, S, LANES), F32)],
        compiler_params=_cparams(("parallel", "parallel")),
        name="nsa_proj",
    )(pb, pos3, invf)


def _nsa_cmp_kernel(t_ref, ptop_ref, pbot_ref, wtop_ref, wbot_ref, b1_ref, w2k_ref, w2v_ref, kc_ref, vc_ref, sh_ref):
    n = t_ref.shape[1]
    t = t_ref[0]
    top = _dot((t + ptop_ref[...]).astype(BF16), wtop_ref[...])
    sh_ref[0:n, :] = _dot((t + pbot_ref[...]).astype(BF16), wbot_ref[...])
    sh_ref[n:n + 8, :] = jnp.zeros((8, sh_ref.shape[1]), F32)
    hid = jax.nn.gelu(top + sh_ref[pl.ds(1, n), :] + b1_ref[...])
    kc_ref[0] = _dot(hid[:, 0:CMP_HIDDEN].astype(BF16), w2k_ref[...]).astype(BF16)
    vc_ref[0] = _dot(hid[:, CMP_HIDDEN:2 * CMP_HIDDEN].astype(BF16), w2v_ref[...]).astype(BF16)


def _nsa_compress(t16, ptop, pbot, wtop, wbot, b1, w2k, w2v):
    B, n, F = t16.shape
    full = lambda shape: pl.BlockSpec(shape, lambda b: (0,) * len(shape))
    ospec = pl.BlockSpec((1, n, LANES), lambda b: (b, 0, 0))
    oshape = jax.ShapeDtypeStruct((B, n, LANES), BF16)
    return pl.pallas_call(
        _nsa_cmp_kernel,
        grid=(B,),
        in_specs=[pl.BlockSpec((1, n, F), lambda b: (b, 0, 0)), full((1, F)), full((1, F)),
                  full((F, 2 * CMP_HIDDEN)), full((F, 2 * CMP_HIDDEN)), full((1, 2 * CMP_HIDDEN)),
                  full((CMP_HIDDEN, LANES)), full((CMP_HIDDEN, LANES))],
        out_specs=[ospec, ospec],
        out_shape=[oshape, oshape],
        scratch_shapes=[pltpu.VMEM((n + 8, 2 * CMP_HIDDEN), F32)],
        compiler_params=_cparams(("parallel",)),
        name="nsa_compress",
    )(t16, ptop, pbot, wtop, wbot, b1, w2k, w2v)


def _masked_softmax(s, mask):
    s = jnp.where(mask, s, NEG_INF)
    m = jnp.max(s, axis=-1, keepdims=True)
    p = jnp.where(mask, jnp.exp(s - m), 0.0)
    return p / jnp.maximum(jnp.sum(p, axis=-1, keepdims=True), 1e-30)


def _nsa_attn_kernel(q_ref, g_ref, kc_ref, vc_ref, ovl_ref, ksa_ref, vs_ref, kw_ref, vw_ref, o_ref,
                     m_ref, l_ref, acc_ref):
    H, TQ, TK = NSA_HEADS, NSA_TQ, NSA_TK
    M = H * TQ
    NB = MAX_SEL_BLOCKS
    s0 = pl.program_id(1) * TQ
    ncmp = kc_ref.shape[1]
    S = ksa_ref.shape[1]
    q = q_ref[0].reshape(M, LANES)
    t_rows = s0 + lax.broadcasted_iota(jnp.int32, (M, 1), 0) % TQ

    cend = lax.broadcasted_iota(jnp.int32, (M, ncmp), 1) * CMP_STRIDE + (CMP_LEN - 1)
    p_c = _masked_softmax(_dot_nt(q, kc_ref[0]), cend <= t_rows)
    o_c = _dot(p_c.astype(BF16), vc_ref[0])

    psum = p_c[0:TQ] + p_c[TQ:2 * TQ] + p_c[2 * TQ:3 * TQ] + p_c[3 * TQ:4 * TQ]
    p_hi = psum.astype(BF16)
    p_lo = (psum - p_hi.astype(F32)).astype(BF16)
    imp = _dot_nt(ovl_ref[...], p_hi) + _dot_nt(ovl_ref[...], p_lo)
    jb = lax.broadcasted_iota(jnp.int32, (NB, TQ), 0)
    tq_l = s0 + lax.broadcasted_iota(jnp.int32, (NB, TQ), 1)
    cur = lax.shift_right_logical(tq_l, 6)
    forced = (jb == 0) | (jb == cur) | (jb == cur - 1)
    score = jnp.where(forced, FORCE_SCORE, jnp.where(jb * SEL_LEN <= tq_l, imp, NEG_INF))
    jbf = jb.astype(F32)
    sel = jnp.zeros((NB, TQ), F32)
    for _ in range(SEL_TOPN):
        best = jnp.max(score, axis=0, keepdims=True)
        first = jnp.min(jnp.where(score == best, jbf, float(NB)), axis=0, keepdims=True)
        hit = jbf == first
        sel = jnp.where(hit, 1.0, sel)
        score = jnp.where(hit, -3e38, score)
    bias = ((sel.T - 1.0) * (-NEG_INF)).astype(BF16)
    q_aug = jnp.concatenate([q, jnp.concatenate([bias] * H, axis=0)], axis=1)

    m_ref[...] = jnp.full(m_ref.shape, NEG_INF, F32)
    l_ref[...] = jnp.zeros(l_ref.shape, F32)
    acc_ref[...] = jnp.zeros(acc_ref.shape, F32)
    n_full = s0 // TK

    def body(j, carry):
        off = pl.multiple_of(j * TK, TK)
        _softmax_step(q_aug, ksa_ref[0, pl.ds(off, TK), :], vs_ref[0, pl.ds(off, TK), :], m_ref, l_ref, acc_ref, None)
        return carry

    lax.fori_loop(0, n_full, body, 0)
    off = pl.multiple_of(n_full * TK, TK)
    kpos = off + lax.broadcasted_iota(jnp.int32, (M, TK), 1)
    _softmax_step(q_aug, ksa_ref[0, pl.ds(off, TK), :], vs_ref[0, pl.ds(off, TK), :], m_ref, l_ref, acc_ref,
                  kpos <= t_rows)
    o_s = acc_ref[...] / jnp.maximum(l_ref[...], 1e-30)

    WK = WINDOW + TQ
    w0 = pl.multiple_of(jnp.maximum(s0 - WINDOW, 0), TQ)
    wpos = w0 + lax.broadcasted_iota(jnp.int32, (M, WK), 1)
    p_w = _masked_softmax(_dot_nt(q, kw_ref[0, pl.ds(w0, WK), :]), (wpos <= t_rows) & (wpos > t_rows - WINDOW))
    o_w = _dot(p_w.astype(BF16), vw_ref[0, pl.ds(w0, WK), :])

    g = jax.nn.sigmoid(g_ref[0][:, NSA_HEAD_DIM:NSA_HEAD_DIM + 3 * H])
    outs = []
    for h in range(H):
        rows = slice(h * TQ, (h + 1) * TQ)
        o_h = (g[:, 3 * h:3 * h + 1] * o_c[rows] + g[:, 3 * h + 1:3 * h + 2] * o_s[rows]
               + g[:, 3 * h + 2:3 * h + 3] * o_w[rows])
        outs.append(o_h[:, 0:NSA_HEAD_DIM])
    o_ref[0] = jnp.concatenate(outs, axis=1)


def _nsa_attn(q, pb, kc, vc, ovl, ksa, vs, kw, vw):
    B, H, S, _ = q.shape
    TQ = NSA_TQ
    ncmp = kc.shape[1]
    per_b = lambda n, w: pl.BlockSpec((1, n, w), lambda b, i: (b, 0, 0))
    return pl.pallas_call(
        _nsa_attn_kernel,
        grid=(B, S // TQ),
        in_specs=[pl.BlockSpec((1, H, TQ, LANES), lambda b, i: (b, 0, i, 0)),
                  pl.BlockSpec((1, TQ, LANES), lambda b, i: (b, i, WB // LANES - 1)),
                  per_b(ncmp, LANES), per_b(ncmp, LANES),
                  pl.BlockSpec((MAX_SEL_BLOCKS, ncmp), lambda b, i: (0, 0)),
                  per_b(S, 2 * LANES), per_b(S, LANES), per_b(S, LANES), per_b(S, LANES)],
        out_specs=pl.BlockSpec((1, TQ, H * NSA_HEAD_DIM), lambda b, i: (b, i, 0)),
        out_shape=jax.ShapeDtypeStruct((B, S, H * NSA_HEAD_DIM), F32),
        scratch_shapes=[pltpu.VMEM((H * TQ, 1), F32), pltpu.VMEM((H * TQ, 1), F32),
                        pltpu.VMEM((H * TQ, LANES), F32)],
        compiler_params=_cparams(("parallel", "arbitrary")),
        name="nsa_attn",
    )(q, pb, kc, vc, ovl, ksa, vs, kw, vw)


def _nsa_overlap(ncmp):
    n = np.arange(ncmp)[None, :] * CMP_STRIDE
    s = np.arange(MAX_SEL_BLOCKS)[:, None] * SEL_LEN
    real = np.arange(ncmp)[None, :] < ncmp - 1
    ovl = (n < s + SEL_LEN) & (n + CMP_LEN > s) & real
    return jnp.asarray(ovl.astype(np.float32), dtype=BF16)


def _outproj_kernel(ya_ref, yb_ref, yc_ref, yd_ref, x_ref, gn_ref, w_ref, o_ref):
    acc = x_ref[...]
    for g, y_ref in enumerate((ya_ref, yb_ref, yc_ref, yd_ref)):
        sl = slice(g * GROUP_WIDTH, (g + 1) * GROUP_WIDTH)
        y = (_rms(y_ref[...], GROUP_WIDTH) * gn_ref[:, sl]).astype(BF16)
        acc = acc + _dot(y, w_ref[sl, :])
    o_ref[...] = acc


def _outproj(ys, x2, gn, w_out_b, tm=512):
    T = x2.shape[0]
    W = GROUP_WIDTH
    return pl.pallas_call(
        _outproj_kernel,
        grid=(T // tm,),
        in_specs=[pl.BlockSpec((tm, W), lambda i: (i, 0))] * 4
        + [pl.BlockSpec((tm, D_MODEL), lambda i: (i, 0)),
           pl.BlockSpec((1, D_MODEL), lambda i: (0, 0)),
           pl.BlockSpec((D_MODEL, D_MODEL), lambda i: (0, 0))],
        out_specs=pl.BlockSpec((tm, D_MODEL), lambda i: (i, 0)),
        out_shape=jax.ShapeDtypeStruct((T, D_MODEL), F32),
        compiler_params=_cparams(("parallel",)),
        name="outproj",
    )(*ys, x2, gn.reshape(1, D_MODEL), w_out_b)


def _ffn_kernel(x_ref, g_ref, wgu_ref, wd_ref, fn_ref, o_ref, *, final_norm):
    x = x_ref[...]
    h = (_rms(x, D_MODEL) * g_ref[...]).astype(BF16)
    acc = x
    for c in range(D_FF // FF_CHUNK):
        lo = c * FF_CHUNK
        gate = _dot(h, wgu_ref[:, lo:lo + FF_CHUNK])
        up = _dot(h, wgu_ref[:, D_FF + lo:D_FF + lo + FF_CHUNK])
        acc = acc + _dot((jax.nn.silu(gate) * up).astype(BF16), wd_ref[lo:lo + FF_CHUNK, :])
    if final_norm:
        acc = _rms(acc, D_MODEL) * fn_ref[...]
    o_ref[...] = acc


def _ffn(x2, gain, wgu_b, wd_b, fnorm, final_norm, tm=256):
    T = x2.shape[0]
    row = pl.BlockSpec((1, D_MODEL), lambda i: (0, 0))
    return pl.pallas_call(
        functools.partial(_ffn_kernel, final_norm=final_norm),
        grid=(T // tm,),
        in_specs=[pl.BlockSpec((tm, D_MODEL), lambda i: (i, 0)), row,
                  pl.BlockSpec((D_MODEL, 2 * D_FF), lambda i: (0, 0)),
                  pl.BlockSpec((D_FF, D_MODEL), lambda i: (0, 0)), row],
        out_specs=pl.BlockSpec((tm, D_MODEL), lambda i: (i, 0)),
        out_shape=jax.ShapeDtypeStruct((T, D_MODEL), F32),
        compiler_params=_cparams(("parallel",)),
        name="ffn",
    )(x2, gain.reshape(1, D_MODEL), wgu_b, wd_b, fnorm.reshape(1, D_MODEL))


def _gather_cols(w, layout):
    parts = []
    for name, width in layout:
        if name is None:
            parts.append(jnp.zeros((w.shape[0], width), w.dtype))
        else:
            o, n = _OFF[name]
            assert n == width
            parts.append(w[:, o:o + n])
    return jnp.concatenate(parts, axis=1)


def _block_diag(w):
    h, d, _ = w.shape
    out = jnp.zeros((h * d, h * d), w.dtype)
    for i in range(h):
        out = out.at[i * d:(i + 1) * d, i * d:(i + 1) * d].set(w[i])
    return out


def _pad_to(a, shape):
    return jnp.pad(a, [(0, s - d) for s, d in zip(shape, a.shape)])


def _mla_weights(w_uq, w_ukv):
    H = MLA_HEADS
    wq = w_uq.reshape(MLA_Q_RANK, H, MLA_QK_DIM)
    wq = _pad_to(wq, (256, H, LANES)).reshape(256, H * LANES)
    wkv = w_ukv.reshape(MLA_KV_RANK, H, MLA_NOPE_DIM + MLA_V_DIM)
    wk = _pad_to(wkv[:, :, :MLA_NOPE_DIM], (MLA_KV_RANK, H, LANES)).reshape(MLA_KV_RANK, H * LANES)
    wv = _pad_to(wkv[:, :, MLA_NOPE_DIM:], (MLA_KV_RANK, H, LANES)).reshape(MLA_KV_RANK, H * LANES)
    return wq.astype(BF16), jnp.concatenate([wk, wv], axis=1).astype(BF16)


def _cmp_weights(cmp_pos, cmp_w1, cmp_b1, cmp_w2):
    half = CMP_LEN // 2
    Dh = NSA_HEAD_DIM
    w1 = cmp_w1.reshape(2, CMP_LEN, Dh, CMP_HIDDEN)

    def rows(part):
        wk = w1[0, part * half:(part + 1) * half]
        wv = w1[1, part * half:(part + 1) * half]
        z = jnp.zeros_like(wk)
        top = jnp.concatenate([wk, z], axis=-1)
        bot = jnp.concatenate([z, wv], axis=-1)
        return jnp.concatenate([top, bot], axis=1).reshape(half * 2 * Dh, 2 * CMP_HIDDEN)

    def pos_row(part):
        p = jnp.concatenate([cmp_pos[0, part * half:(part + 1) * half], cmp_pos[1, part * half:(part + 1) * half]],
                            axis=-1)
        return p.reshape(1, half * 2 * Dh)

    b1 = jnp.concatenate([cmp_b1[0], cmp_b1[1]]).reshape(1, 2 * CMP_HIDDEN)
    w2k = _pad_to(cmp_w2[0], (CMP_HIDDEN, LANES)).astype(BF16)
    w2v = _pad_to(cmp_w2[1], (CMP_HIDDEN, LANES)).astype(BF16)
    return pos_row(0), pos_row(1), rows(0).astype(BF16), rows(1).astype(BF16), b1, w2k, w2v


def kernel(x, positions, norm_mix, w_in, conv_w, conv_b, lru_wa, lru_ba, lru_wx, lru_bx, lru_lambda, cmp_pos, cmp_w1, cmp_b1, cmp_w2, gla_wg2, gla_bg2, gla_norm, mla_q_norm, mla_kv_norm, mla_w_uq, mla_w_ukv, group_norm, w_out, norm_ffn, w_gate_up, w_down, final_norm):
    B, S, D = x.shape
    assert D == D_MODEL and S % (CMP_STRIDE * 8) == 0 and S // SEL_LEN <= MAX_SEL_BLOCKS and S >= WINDOW + NSA_TQ
    depth = w_in.shape[0]
    T = B * S
    pos3 = positions.reshape(B, S, 1)
    ncmp = S // CMP_STRIDE
    ovl = _nsa_overlap(ncmp)
    x2 = x.reshape(T, D)
    for l in range(depth):
        w_p = jnp.concatenate([_gather_cols(w_in[l], lay) for lay in (_LAYOUT_A, _LAYOUT_B, _LAYOUT_C, _LAYOUT_D)],
                              axis=1).astype(BF16)
        pa, pb, pc, pd = _inproj(x2, norm_mix[l], w_p)
        pa, pb, pc, pd = (p.reshape(B, S, -1) for p in (pa, pb, pc, pd))

        wg = jnp.concatenate([_block_diag(lru_wa[l]), _block_diag(lru_wx[l])], axis=1).astype(BF16)
        bg = jnp.concatenate([lru_ba[l], lru_bx[l]]).reshape(1, 2 * GROUP_WIDTH)
        y_a = _rglru(pa, conv_w[l], conv_b[l], wg, bg, lru_lambda[l])

        q_n, ksa, vs, kw, vw, kcvc = _nsa_proj(pb, pos3)
        t16 = kcvc.reshape(B, ncmp, CMP_STRIDE * 2 * NSA_HEAD_DIM)
        kc, vc = _nsa_compress(t16, *_cmp_weights(cmp_pos[l], cmp_w1[l], cmp_b1[l], cmp_w2[l]))
        y_b = _nsa_attn(q_n, pb, kc, vc, ovl, ksa, vs, kw, vw)

        wg2_p = _pad_to(gla_wg2[l], (LANES, GLA_HEADS * GLA_DK))
        y_c = _gla(pc, wg2_p, gla_bg2[l], jnp.tile(gla_norm[l], GLA_HEADS))

        wq_p, wkv_p = _mla_weights(mla_w_uq[l], mla_w_ukv[l])
        q_m, k_m, v_m = _mla_proj(pd, pos3, _pad_to(mla_q_norm[l], (256,)).reshape(1, 256), mla_kv_norm[l],
                                  wq_p, wkv_p)
        y_d = _mla_attn(q_m, k_m, v_m)

        ys = [y.reshape(T, GROUP_WIDTH) for y in (y_a, y_b, y_c, y_d)]
        x2 = _outproj(ys, x2, group_norm[l], w_out[l].astype(BF16))
        x2 = _ffn(x2, norm_ffn[l], w_gate_up[l].astype(BF16), w_down[l].astype(BF16), final_norm,
                  final_norm=(l == depth - 1))
    return x2.reshape(B, S, D)
```

```python
import functools

import numpy as np
import jax
import jax.numpy as jnp
from jax import lax
from jax.experimental import pallas as pl
from jax.experimental.pallas import tpu as pltpu

F32 = jnp.float32
BF16 = jnp.bfloat16
HIGHEST = lax.Precision.HIGHEST

D_MODEL = 1024
GROUP_WIDTH = 256
ROPE_THETA = 10000.0
NORM_EPS = 1e-6
NEG_INF = -1e30
FORCE_SCORE = 1e9

LRU_C = 8.0
CONV_WIDTH = 4

NSA_HEADS = 4
NSA_HEAD_DIM = 64
CMP_LEN = 32
CMP_STRIDE = 16
CMP_HIDDEN = 256
SEL_LEN = 64
SEL_TOPN = 16
WINDOW = 512
NSA_TQ = 128
NSA_TK = 512
MAX_SEL_BLOCKS = 128

GLA_HEADS = 4
GLA_DV = 64
GLA_DK = 32
GLA_GATE_RANK = 16
GLA_TAU = 16.0
GLA_TILE = 128
GLA_SUB = 16

MLA_HEADS = 4
MLA_V_DIM = 64
MLA_NOPE_DIM = 64
MLA_ROPE_DIM = 32
MLA_QK_DIM = 96
MLA_Q_RANK = 192
MLA_KV_RANK = 128
MLA_TQ = 256
MLA_TK = 512

D_FF = 2816
FF_CHUNK = 256

LANES = 128
V_ONES = 64
LOG2E = 1.4426950408889634
VMEM_LIMIT = 56 * 1024 * 1024

_OFF = {}
_o = 0
for _n, _w in (("a_x", 256), ("a_gate", 256), ("b_q", 256), ("k_c", 64), ("v_c", 64), ("k_s", 64), ("v_s", 64),
               ("k_w", 64), ("v_w", 64), ("b_gate", 12), ("c_q", 128), ("c_k", 128), ("c_v", 256), ("c_glr", 16),
               ("c_og", 256), ("d_cq", 192), ("d_ckv", 128), ("d_kr", 32)):
    _OFF[_n] = (_o, _w)
    _o += _w
IN_COLS = _o
_LAYOUT_A = (("a_x", 256), ("a_gate", 256))
_LAYOUT_B = (("b_q", 256), ("k_s", 64), ("k_c", 64), ("k_w", 64), (None, 64),
             ("v_c", 64), ("v_s", 64), ("v_w", 64), ("b_gate", 12), (None, 52))
_LAYOUT_C = (("c_q", 128), ("c_k", 128), ("c_v", 256), ("c_og", 256), ("c_glr", 16), (None, 112))
_LAYOUT_D = (("d_cq", 192), (None, 64), ("d_ckv", 128), (None, 64), ("d_kr", 32), (None, 32))
WA, WB, WC, WD = 512, 768, 896, 512


def _cparams(sem):
    return pltpu.CompilerParams(dimension_semantics=sem, vmem_limit_bytes=VMEM_LIMIT)


def _dot(a, b):
    return jnp.dot(a, b, preferred_element_type=F32)


def _dot_nt(a, b):
    return lax.dot_general(a, b, (((1,), (1,)), ((), ())), preferred_element_type=F32)


def _dot_tn(a, b):
    return lax.dot_general(a, b, (((0,), (0,)), ((), ())), preferred_element_type=F32)


def _dot_exact(a, b):
    return jnp.dot(a, b, precision=HIGHEST, preferred_element_type=F32)


def _rms(x, width):
    return x * lax.rsqrt(jnp.sum(x * x, axis=-1, keepdims=True) / width + NORM_EPS)


def _inproj_kernel(x_ref, g_ref, w_ref, oa_ref, ob_ref, oc_ref, od_ref):
    h = (_rms(x_ref[...], D_MODEL) * g_ref[...]).astype(BF16)
    off = 0
    for o_ref in (oa_ref, ob_ref, oc_ref, od_ref):
        w = o_ref.shape[1]
        o_ref[...] = _dot(h, w_ref[:, off:off + w])
        off += w


def _inproj(x2, gain, w_p, tm=512):
    T = x2.shape[0]
    ntot = w_p.shape[1]
    return pl.pallas_call(
        _inproj_kernel,
        grid=(T // tm,),
        in_specs=[pl.BlockSpec((tm, D_MODEL), lambda i: (i, 0)),
                  pl.BlockSpec((1, D_MODEL), lambda i: (0, 0)),
                  pl.BlockSpec((D_MODEL, ntot), lambda i: (0, 0))],
        out_specs=[pl.BlockSpec((tm, w), lambda i: (i, 0)) for w in (WA, WB, WC, WD)],
        out_shape=[jax.ShapeDtypeStruct((T, w), F32) for w in (WA, WB, WC, WD)],
        compiler_params=_cparams(("parallel",)),
        name="inproj",
    )(x2, gain.reshape(1, D_MODEL), w_p)


def _rglru_kernel(pa_ref, cw_ref, cb_ref, wg_ref, bg_ref, lam_ref, o_ref, xbuf, a_s, u_s, h_s, hlast):
    ts = pa_ref.shape[1]
    W = GROUP_WIDTH

    @pl.when(pl.program_id(1) == 0)
    def _():
        xbuf[0:8, :] = jnp.zeros((8, W), F32)
        hlast[...] = jnp.zeros_like(hlast)

    xbuf[8:8 + ts, :] = pa_ref[0, :, 0:W]
    xc = cb_ref[...]
    for k in range(CONV_WIDTH):
        lo = 8 - (CONV_WIDTH - 1) + k
        xc = xc + cw_ref[k:k + 1, :] * xbuf[lo:lo + ts, :]
    xbuf[0:8, :] = xbuf[ts:ts + 8, :]
    gates = _dot(xc.astype(BF16), wg_ref[...]) + bg_ref[...]
    r = jax.nn.sigmoid(gates[:, 0:W])
    i = jax.nn.sigmoid(gates[:, W:2 * W])
    log_a = (-LRU_C) * r * jax.nn.softplus(-lam_ref[...])
    a = jnp.exp(log_a)
    u = jnp.sqrt(-jnp.tanh(log_a) * (a * a + 1.0)) * (i * xc)
    a_s[...] = a
    u_s[...] = u

    def body(t, h):
        h = a_s[pl.ds(t, 1), :] * h + u_s[pl.ds(t, 1), :]
        h_s[pl.ds(t, 1), :] = h
        return h

    hlast[...] = lax.fori_loop(0, ts, body, hlast[...], unroll=8)
    o_ref[0] = h_s[...] * jax.nn.gelu(pa_ref[0, :, W:2 * W])


def _rglru(pa, conv_w, conv_b, wg, bg, lam, ts=512):
    B, S, _ = pa.shape
    W = GROUP_WIDTH
    full = lambda shape: pl.BlockSpec(shape, lambda b, j: (0,) * len(shape))
    return pl.pallas_call(
        _rglru_kernel,
        grid=(B, S // ts),
        in_specs=[pl.BlockSpec((1, ts, WA), lambda b, j: (b, j, 0)),
                  full((CONV_WIDTH, W)), full((1, W)), full((W, 2 * W)), full((1, 2 * W)), full((1, W))],
        out_specs=pl.BlockSpec((1, ts, W), lambda b, j: (b, j, 0)),
        out_shape=jax.ShapeDtypeStruct((B, S, W), F32),
        scratch_shapes=[pltpu.VMEM((ts + 8, W), F32), pltpu.VMEM((ts, W), F32), pltpu.VMEM((ts, W), F32),
                        pltpu.VMEM((ts, W), F32), pltpu.VMEM((1, W), F32)],
        compiler_params=_cparams(("parallel", "arbitrary")),
        name="rglru",
    )(pa, conv_w, conv_b.reshape(1, W), wg, bg, lam.reshape(1, W))


def _gla_kernel(pc_ref, wg2_ref, bg2_ref, gn_ref, lcum_ref, lsum_ref, hsum_ref, gmean_ref, smask_ref, o_ref, st_ref):
    TT, SUB = GLA_TILE, GLA_SUB
    NB = TT // SUB
    KW = GLA_HEADS * GLA_DK
    VW = GLA_HEADS * GLA_DV

    @pl.when(pl.program_id(1) == 0)
    def _():
        st_ref[...] = jnp.zeros_like(st_ref)

    q = pc_ref[0, :, 0:KW] * (GLA_DK ** -0.5)
    k = pc_ref[0, :, KW:2 * KW]
    v = pc_ref[0, :, 2 * KW:2 * KW + VW]
    og = pc_ref[0, :, 2 * KW + VW:2 * KW + 2 * VW]
    glr = pc_ref[0, :, 2 * KW + 2 * VW:2 * KW + 2 * VW + LANES]
    log_a = jax.nn.log_sigmoid(_dot_exact(glr, wg2_ref[...]) + bg2_ref[...]) * (1.0 / GLA_TAU)
    b = _dot_exact(lcum_ref[...], log_a)
    bl = _dot_exact(lsum_ref[...], log_a)

    q3 = q.reshape(NB, SUB, KW)
    k3 = k.reshape(NB, SUB, KW)
    b3 = b.reshape(NB, SUB, KW)
    v3 = v.reshape(NB, SUB, VW)
    row = lax.broadcasted_iota(jnp.int32, (NB, SUB, KW), 1)
    terms = []
    for j in range(SUB):
        kj = jnp.broadcast_to(k3[:, j:j + 1, :], (NB, SUB, KW))
        bj = jnp.broadcast_to(b3[:, j:j + 1, :], (NB, SUB, KW))
        e = q3 * kj * jnp.exp(jnp.where(row >= j, b3 - bj, NEG_INF))
        terms.append(e.reshape(TT, KW))
    e_all = jnp.concatenate(terms, axis=0)
    e_hi = e_all.astype(BF16)
    e_lo = (e_all - e_hi.astype(F32)).astype(BF16)
    attn = _dot(e_hi, hsum_ref[...]) + _dot(e_lo, hsum_ref[...])
    o = jnp.zeros((NB, SUB, VW), F32)
    for j in range(SUB):
        vj = jnp.broadcast_to(v3[:, j:j + 1, :], (NB, SUB, VW))
        o = o + attn[j * TT:(j + 1) * TT, :].reshape(NB, SUB, VW) * vj
    o = o.reshape(TT, VW)

    qd = (q * jnp.exp(b)).astype(BF16)
    kd = (k * jnp.exp(bl - b)).astype(BF16)
    dec = jnp.exp(bl)
    vb = v.astype(BF16)
    st = st_ref[...]
    inter = []
    for n in range(NB):
        rows = slice(n * SUB, (n + 1) * SUB)
        inter.append(_dot_nt(qd[rows], st.astype(BF16)))
        st = st * dec[n * SUB:n * SUB + 1, :] + _dot_tn(vb[rows], kd[rows]) * smask_ref[...]
    st_ref[...] = st
    o = o + jnp.concatenate(inter, axis=0)

    ms = _dot_exact(o * o, gmean_ref[...])
    o_ref[0] = o * lax.rsqrt(ms + NORM_EPS) * gn_ref[...] * jax.nn.silu(og)


def _gla_consts():
    TT, SUB = GLA_TILE, GLA_SUB
    KW, VW = GLA_HEADS * GLA_DK, GLA_HEADS * GLA_DV
    i = np.arange(TT)
    same = (i[:, None] // SUB) == (i[None, :] // SUB)
    lcum = (same & (i[None, :] <= i[:, None])).astype(np.float32)
    lsum = same.astype(np.float32)
    hk = np.arange(KW) // GLA_DK
    hv = np.arange(VW) // GLA_DV
    hsum = (hk[:, None] == hv[None, :]).astype(np.float32)
    gmean = (hv[:, None] == hv[None, :]).astype(np.float32) / GLA_DV
    smask = (hv[:, None] == hk[None, :]).astype(np.float32)
    return (jnp.asarray(lcum), jnp.asarray(lsum), jnp.asarray(hsum, dtype=BF16), jnp.asarray(gmean),
            jnp.asarray(smask))


def _gla(pc, wg2_p, bg2, gn_t):
    B, S, _ = pc.shape
    TT = GLA_TILE
    KW, VW = GLA_HEADS * GLA_DK, GLA_HEADS * GLA_DV
    lcum, lsum, hsum, gmean, smask = _gla_consts()
    full = lambda shape: pl.BlockSpec(shape, lambda b, j: (0,) * len(shape))
    return pl.pallas_call(
        _gla_kernel,
        grid=(B, S // TT),
        in_specs=[pl.BlockSpec((1, TT, WC), lambda b, j: (b, j, 0)),
                  full((LANES, KW)), full((1, KW)), full((1, VW)), full((TT, TT)), full((TT, TT)),
                  full((KW, VW)), full((VW, VW)), full((VW, KW))],
        out_specs=pl.BlockSpec((1, TT, VW), lambda b, j: (b, j, 0)),
        out_shape=jax.ShapeDtypeStruct((B, S, VW), F32),
        scratch_shapes=[pltpu.VMEM((VW, KW), F32)],
        compiler_params=_cparams(("parallel", "arbitrary")),
        name="gla",
    )(pc, wg2_p, bg2.reshape(1, KW), gn_t.reshape(1, VW), lcum, lsum, hsum, gmean, smask)


def _rope_lanes(x, pos_f, invf_row, half):
    outs = []
    for c in range(x.shape[1] // LANES):
        xs = x[:, c * LANES:(c + 1) * LANES]
        ang = pos_f * invf_row[:, c * LANES:(c + 1) * LANES]
        lane = lax.broadcasted_iota(jnp.int32, xs.shape, 1)
        lo = (lane % (2 * half)) < half
        rot = jnp.where(lo, -pltpu.roll(xs, LANES - half, 1), pltpu.roll(xs, half, 1))
        outs.append(xs * jnp.cos(ang) + rot * jnp.sin(ang))
    return outs[0] if len(outs) == 1 else jnp.concatenate(outs, axis=1)


def _inv_freq_row(d, reps, lead_zeros=0, tail_zeros=0):
    inv = ROPE_THETA ** (-jnp.arange(0, d, 2, dtype=F32) / d)
    row = jnp.concatenate([jnp.zeros((lead_zeros,), F32), jnp.tile(jnp.concatenate([inv, inv]), reps),
                           jnp.zeros((tail_zeros,), F32)])
    return row.reshape(1, -1)


def _mla_proj_kernel(pd_ref, pos_ref, qn_ref, kvn_ref, wq_ref, wkv_ref, invf_ref, q_ref, kt_ref, v_ref):
    H = MLA_HEADS
    tm = pd_ref.shape[1]
    pos_f = pos_ref[0].astype(F32)
    cq = (_rms(pd_ref[0, :, 0:256], MLA_Q_RANK) * qn_ref[...]).astype(BF16)
    ckv = (_rms(pd_ref[0, :, 256:384], MLA_KV_RANK) * kvn_ref[...]).astype(BF16)
    kr = _rope_lanes(pd_ref[0, :, 384:512], pos_f, invf_ref[...], MLA_ROPE_DIM // 2)
    q_all = _dot(cq, wq_ref[...])
    kv_all = _dot(ckv, wkv_ref[...])
    low = lax.broadcasted_iota(jnp.int32, (tm, LANES), 1) < MLA_V_DIM
    for h in range(H):
        qh = _rope_lanes(q_all[:, h * LANES:(h + 1) * LANES], pos_f, invf_ref[...], MLA_ROPE_DIM // 2)
        q_ref[0, h] = (qh * (MLA_QK_DIM ** -0.5 * LOG2E)).astype(BF16)
        kt_ref[0, h, 0] = (kv_all[:, h * LANES:(h + 1) * LANES] + kr).T.astype(BF16)
        v_ref[0, h] = jnp.where(low, kv_all[:, (H + h) * LANES:(H + h + 1) * LANES], 1.0).astype(BF16)


def _mla_proj(pd, pos3, qn_p, kvn, wq_p, wkv_p):
    B, S, _ = pd.shape
    H = MLA_HEADS
    tm = MLA_TK
    invf = _inv_freq_row(MLA_ROPE_DIM, 1, lead_zeros=64, tail_zeros=32)
    full = lambda shape: pl.BlockSpec(shape, lambda b, j: (0,) * len(shape))
    hspec = pl.BlockSpec((1, H, tm, LANES), lambda b, j: (b, 0, j, 0))
    hshape = jax.ShapeDtypeStruct((B, H, S, LANES), BF16)
    return pl.pallas_call(
        _mla_proj_kernel,
        grid=(B, S // tm),
        in_specs=[pl.BlockSpec((1, tm, WD), lambda b, j: (b, j, 0)),
                  pl.BlockSpec((1, tm, 1), lambda b, j: (b, j, 0)),
                  full((1, 256)), full((1, LANES)), full((256, H * LANES)), full((LANES, 2 * H * LANES)),
                  full((1, LANES))],
        out_specs=[hspec, pl.BlockSpec((1, H, 1, LANES, tm), lambda b, j: (b, 0, j, 0, 0)), hspec],
        out_shape=[hshape, jax.ShapeDtypeStruct((B, H, S // tm, LANES, tm), BF16), hshape],
        compiler_params=_cparams(("parallel", "parallel")),
        name="mla_proj",
    )(pd, pos3, qn_p, kvn.reshape(1, LANES), wq_p, wkv_p, invf)


def _softmax_step(q, kt, v1, m_ref, acc_ref, mask):
    s = _dot(q, kt)
    if mask is not None:
        s = jnp.where(mask, s, NEG_INF)
    m_prev = m_ref[...]
    m_new = jnp.maximum(m_prev, jnp.max(s, axis=-1, keepdims=True))
    acc_ref[...] = jnp.exp2(m_prev - m_new) * acc_ref[...] + _dot(jnp.exp2(s - m_new).astype(BF16), v1)
    m_ref[...] = m_new


def _normalize(acc):
    return acc / jnp.maximum(acc[:, V_ONES:V_ONES + 1], 1e-30)


def _mla_attn_kernel(q_ref, kt_ref, v_ref, o_ref, m_ref, acc_ref):
    tq, tk = MLA_TQ, MLA_TK
    s0 = pl.program_id(1) * tq
    n_full = s0 // tk
    m_ref[...] = jnp.full(m_ref.shape, NEG_INF, F32)
    acc_ref[...] = jnp.zeros(acc_ref.shape, F32)

    def step(j, mask):
        off = pl.multiple_of(j * tk, tk)
        for h in range(MLA_HEADS):
            _softmax_step(q_ref[0, h], kt_ref[0, h, j], v_ref[0, h, pl.ds(off, tk), :],
                          m_ref.at[h], acc_ref.at[h], mask)

    def body(j, carry):
        step(j, None)
        return carry

    lax.fori_loop(0, n_full, body, 0)
    t = s0 + lax.broadcasted_iota(jnp.int32, (tq, tk), 0)
    kpos = n_full * tk + lax.broadcasted_iota(jnp.int32, (tq, tk), 1)
    step(n_full, kpos <= t)
    o_ref[0] = jnp.concatenate([_normalize(acc_ref[h])[:, 0:MLA_V_DIM] for h in range(MLA_HEADS)], axis=1)


def _mla_attn(q, kt, v):
    B, H, S, _ = q.shape
    tq, tk = MLA_TQ, MLA_TK
    return pl.pallas_call(
        _mla_attn_kernel,
        grid=(B, S // tq),
        in_specs=[pl.BlockSpec((1, H, tq, LANES), lambda b, i: (b, 0, i, 0)),
                  pl.BlockSpec((1, H, S // tk, LANES, tk), lambda b, i: (b, 0, 0, 0, 0)),
                  pl.BlockSpec((1, H, S, LANES), lambda b, i: (b, 0, 0, 0))],
        out_specs=pl.BlockSpec((1, tq, H * MLA_V_DIM), lambda b, i: (b, i, 0)),
        out_shape=jax.ShapeDtypeStruct((B, S, H * MLA_V_DIM), F32),
        scratch_shapes=[pltpu.VMEM((H, tq, 1), F32), pltpu.VMEM((H, tq, LANES), F32)],
        compiler_params=_cparams(("parallel", "arbitrary")),
        name="mla_attn",
    )(q, kt, v)


def _nsa_proj_kernel(pb_ref, pos_ref, invf_ref, q_ref, kst_ref, vs_ref, kw_ref, vw_ref, kcvc_ref):
    tm = pb_ref.shape[1]
    H = NSA_HEADS
    pos = pos_ref[0]
    r = _rope_lanes(pb_ref[0, :, 0:512], pos.astype(F32), invf_ref[...], NSA_HEAD_DIM // 2)
    nr0 = pb_ref[0, :, 512:640]
    nr1 = pb_ref[0, :, 640:768]
    lane = lax.broadcasted_iota(jnp.int32, (tm, LANES), 1)
    low = lane < NSA_HEAD_DIM
    scale = NSA_HEAD_DIM ** -0.5 * LOG2E
    for h in range(H):
        seg = r[:, (h // 2) * LANES:(h // 2 + 1) * LANES]
        if h % 2:
            seg = pltpu.roll(seg, NSA_HEAD_DIM, 1)
        q_ref[0, h] = jnp.where(low, seg * scale, 0.0).astype(BF16)
    kseg = r[:, 256:384]
    blk = lax.shift_right_logical(lax.broadcasted_iota(jnp.int32, (tm, LANES), 0) + pl.program_id(1) * tm, 6)
    kst_ref[0, 0, 0:LANES, :] = jnp.where(low, kseg, 0.0).T.astype(BF16)
    kst_ref[0, 0, LANES:2 * LANES, :] = jnp.where(blk == lane, 1.0, 0.0).T.astype(BF16)
    kw_ref[0] = r[:, 384:512].astype(BF16)
    vs_ref[0] = jnp.where(low, pltpu.roll(nr0, NSA_HEAD_DIM, 1), 1.0).astype(BF16)
    vw_ref[0] = jnp.where(low, nr1, 1.0).astype(BF16)
    kcvc_ref[0] = jnp.where(low, pltpu.roll(kseg, NSA_HEAD_DIM, 1), pltpu.roll(nr0, NSA_HEAD_DIM, 1))


def _nsa_proj(pb, pos3):
    B, S, _ = pb.shape
    H = NSA_HEADS
    tm = NSA_TK
    invf = _inv_freq_row(NSA_HEAD_DIM, 8)
    tok = lambda w: pl.BlockSpec((1, tm, w), lambda b, j: (b, j, 0))
    return pl.pallas_call(
        _nsa_proj_kernel,
        grid=(B, S // tm),
        in_specs=[tok(WB), tok(1), pl.BlockSpec((1, 512), lambda b, j: (0, 0))],
        out_specs=[pl.BlockSpec((1, H, tm, LANES), lambda b, j: (b, 0, j, 0)),
                   pl.BlockSpec((1, 1, 2 * LANES, tm), lambda b, j: (b, j, 0, 0)),
                   tok(LANES), tok(LANES), tok(LANES), tok(LANES)],
        out_shape=[jax.ShapeDtypeStruct((B, H, S, LANES), BF16),
                   jax.ShapeDtypeStruct((B, S // tm, 2 * LANES, tm), BF16),
                   jax.ShapeDtypeStruct((B, S, LANES), BF16),
                   jax.ShapeDtypeStruct((B, S, LANES), BF16),
                   jax.ShapeDtypeStruct((B, S, LANES), BF16),
                   jax.ShapeDtypeStruct((B, S, LANES), F32)],
        compiler_params=_cparams(("parallel", "parallel")),
        name="nsa_proj",
    )(pb, pos3, invf)


def _nsa_cmp_kernel(t_ref, ptop_ref, pbot_ref, wtop_ref, wbot_ref, b1_ref, w2k_ref, w2v_ref, kc_ref, vc_ref, sh_ref):
    n = t_ref.shape[1]
    t = t_ref[0]
    top = _dot((t + ptop_ref[...]).astype(BF16), wtop_ref[...])
    sh_ref[0:n, :] = _dot((t + pbot_ref[...]).astype(BF16), wbot_ref[...])
    sh_ref[n:n + 8, :] = jnp.zeros((8, sh_ref.shape[1]), F32)
    hid = jax.nn.gelu(top + sh_ref[pl.ds(1, n), :] + b1_ref[...])
    kc_ref[0] = _dot(hid[:, 0:CMP_HIDDEN].astype(BF16), w2k_ref[...]).astype(BF16)
    vc = _dot(hid[:, CMP_HIDDEN:2 * CMP_HIDDEN].astype(BF16), w2v_ref[...])
    low = lax.broadcasted_iota(jnp.int32, vc.shape, 1) < NSA_HEAD_DIM
    vc_ref[0] = jnp.where(low, vc, 1.0).astype(BF16)


def _nsa_compress(t16, ptop, pbot, wtop, wbot, b1, w2k, w2v):
    B, n, F = t16.shape
    full = lambda shape: pl.BlockSpec(shape, lambda b: (0,) * len(shape))
    ospec = pl.BlockSpec((1, n, LANES), lambda b: (b, 0, 0))
    oshape = jax.ShapeDtypeStruct((B, n, LANES), BF16)
    return pl.pallas_call(
        _nsa_cmp_kernel,
        grid=(B,),
        in_specs=[pl.BlockSpec((1, n, F), lambda b: (b, 0, 0)), full((1, F)), full((1, F)),
                  full((F, 2 * CMP_HIDDEN)), full((F, 2 * CMP_HIDDEN)), full((1, 2 * CMP_HIDDEN)),
                  full((CMP_HIDDEN, LANES)), full((CMP_HIDDEN, LANES))],
        out_specs=[ospec, ospec],
        out_shape=[oshape, oshape],
        scratch_shapes=[pltpu.VMEM((n + 8, 2 * CMP_HIDDEN), F32)],
        compiler_params=_cparams(("parallel",)),
        name="nsa_compress",
    )(t16, ptop, pbot, wtop, wbot, b1, w2k, w2v)


def _masked_exp(s, mask):
    s = jnp.where(mask, s, NEG_INF)
    return jnp.where(mask, jnp.exp2(s - jnp.max(s, axis=-1, keepdims=True)), 0.0)


def _nsa_attn_kernel(q_ref, g_ref, kc_ref, vc_ref, ovl_ref, kst_ref, vs_ref, kw_ref, vw_ref, o_ref,
                     m_ref, acc_ref):
    H, TQ, TK = NSA_HEADS, NSA_TQ, NSA_TK
    M = H * TQ
    NB = MAX_SEL_BLOCKS
    s0 = pl.program_id(1) * TQ
    ncmp = kc_ref.shape[1]
    q = q_ref[0].reshape(M, LANES)
    t_rows = s0 + lax.broadcasted_iota(jnp.int32, (M, 1), 0) % TQ

    cend = lax.broadcasted_iota(jnp.int32, (M, ncmp), 1) * CMP_STRIDE + (CMP_LEN - 1)
    e_c = _masked_exp(_dot_nt(q, kc_ref[0]), cend <= t_rows)
    a_c = _dot(e_c.astype(BF16), vc_ref[0])
    inv_c = 1.0 / jnp.maximum(a_c[:, V_ONES:V_ONES + 1], 1e-30)
    o_c = a_c * inv_c
    p_c = e_c * inv_c

    psum = p_c[0:TQ] + p_c[TQ:2 * TQ] + p_c[2 * TQ:3 * TQ] + p_c[3 * TQ:4 * TQ]
    p_hi = psum.astype(BF16)
    p_lo = (psum - p_hi.astype(F32)).astype(BF16)
    imp = _dot_nt(ovl_ref[...], p_hi) + _dot_nt(ovl_ref[...], p_lo)
    jb = lax.broadcasted_iota(jnp.int32, (NB, TQ), 0)
    tq_l = s0 + lax.broadcasted_iota(jnp.int32, (NB, TQ), 1)
    cur = lax.shift_right_logical(tq_l, 6)
    forced = (jb == 0) | (jb == cur) | (jb == cur - 1)
    score = jnp.where(forced, FORCE_SCORE, jnp.where(jb * SEL_LEN <= tq_l, imp, NEG_INF))
    jbf = jb.astype(F32)
    sel = jnp.zeros((NB, TQ), F32)
    for _ in range(SEL_TOPN):
        best = jnp.max(score, axis=0, keepdims=True)
        first = jnp.min(jnp.where(score == best, jbf, float(NB)), axis=0, keepdims=True)
        hit = jbf == first
        sel = jnp.where(hit, 1.0, sel)
        score = jnp.where(hit, -3e38, score)
    bias = ((sel.T - 1.0) * (-NEG_INF)).astype(BF16)
    q_aug = jnp.concatenate([q, jnp.concatenate([bias] * H, axis=0)], axis=1)

    m_ref[...] = jnp.full(m_ref.shape, NEG_INF, F32)
    acc_ref[...] = jnp.zeros(acc_ref.shape, F32)
    n_full = s0 // TK

    def body(j, carry):
        off = pl.multiple_of(j * TK, TK)
        _softmax_step(q_aug, kst_ref[0, j], vs_ref[0, pl.ds(off, TK), :], m_ref, acc_ref, None)
        return carry

    lax.fori_loop(0, n_full, body, 0)
    off = pl.multiple_of(n_full * TK, TK)
    kpos = off + lax.broadcasted_iota(jnp.int32, (M, TK), 1)
    _softmax_step(q_aug, kst_ref[0, n_full], vs_ref[0, pl.ds(off, TK), :], m_ref, acc_ref, kpos <= t_rows)
    o_s = _normalize(acc_ref[...])

    WK = WINDOW + TQ
    w0 = pl.multiple_of(jnp.maximum(s0 - WINDOW, 0), TQ)
    wpos = w0 + lax.broadcasted_iota(jnp.int32, (M, WK), 1)
    e_w = _masked_exp(_dot_nt(q, kw_ref[0, pl.ds(w0, WK), :]), (wpos <= t_rows) & (wpos > t_rows - WINDOW))
    o_w = _normalize(_dot(e_w.astype(BF16), vw_ref[0, pl.ds(w0, WK), :]))

    g = jax.nn.sigmoid(g_ref[0][:, NSA_HEAD_DIM:NSA_HEAD_DIM + 3 * H])
    outs = []
    for h in range(H):
        rows = slice(h * TQ, (h + 1) * TQ)
        o_h = (g[:, 3 * h:3 * h + 1] * o_c[rows] + g[:, 3 * h + 1:3 * h + 2] * o_s[rows]
               + g[:, 3 * h + 2:3 * h + 3] * o_w[rows])
        outs.append(o_h[:, 0:NSA_HEAD_DIM])
    o_ref[0] = jnp.concatenate(outs, axis=1)


def _nsa_attn(q, pb, kc, vc, ovl, kst, vs, kw, vw):
    B, H, S, _ = q.shape
    TQ, TK = NSA_TQ, NSA_TK
    ncmp = kc.shape[1]
    per_b = lambda n, w: pl.BlockSpec((1, n, w), lambda b, i: (b, 0, 0))
    return pl.pallas_call(
        _nsa_attn_kernel,
        grid=(B, S // TQ),
        in_specs=[pl.BlockSpec((1, H, TQ, LANES), lambda b, i: (b, 0, i, 0)),
                  pl.BlockSpec((1, TQ, LANES), lambda b, i: (b, i, WB // LANES - 1)),
                  per_b(ncmp, LANES), per_b(ncmp, LANES),
                  pl.BlockSpec((MAX_SEL_BLOCKS, ncmp), lambda b, i: (0, 0)),
                  pl.BlockSpec((1, S // TK, 2 * LANES, TK), lambda b, i: (b, 0, 0, 0)),
                  per_b(S, LANES), per_b(S, LANES), per_b(S, LANES)],
        out_specs=pl.BlockSpec((1, TQ, H * NSA_HEAD_DIM), lambda b, i: (b, i, 0)),
        out_shape=jax.ShapeDtypeStruct((B, S, H * NSA_HEAD_DIM), F32),
        scratch_shapes=[pltpu.VMEM((H * TQ, 1), F32), pltpu.VMEM((H * TQ, LANES), F32)],
        compiler_params=_cparams(("parallel", "arbitrary")),
        name="nsa_attn",
    )(q, pb, kc, vc, ovl, kst, vs, kw, vw)


def _nsa_overlap(ncmp):
    n = np.arange(ncmp)[None, :] * CMP_STRIDE
    s = np.arange(MAX_SEL_BLOCKS)[:, None] * SEL_LEN
    real = np.arange(ncmp)[None, :] < ncmp - 1
    ovl = (n < s + SEL_LEN) & (n + CMP_LEN > s) & real
    return jnp.asarray(ovl.astype(np.float32), dtype=BF16)


def _outproj_kernel(ya_ref, yb_ref, yc_ref, yd_ref, x_ref, gn_ref, w_ref, o_ref):
    acc = x_ref[...]
    for g, y_ref in enumerate((ya_ref, yb_ref, yc_ref, yd_ref)):
        sl = slice(g * GROUP_WIDTH, (g + 1) * GROUP_WIDTH)
        y = (_rms(y_ref[...], GROUP_WIDTH) * gn_ref[:, sl]).astype(BF16)
        acc = acc + _dot(y, w_ref[sl, :])
    o_ref[...] = acc


def _outproj(ys, x2, gn, w_out_b, tm=512):
    T = x2.shape[0]
    W = GROUP_WIDTH
    return pl.pallas_call(
        _outproj_kernel,
        grid=(T // tm,),
        in_specs=[pl.BlockSpec((tm, W), lambda i: (i, 0))] * 4
        + [pl.BlockSpec((tm, D_MODEL), lambda i: (i, 0)),
           pl.BlockSpec((1, D_MODEL), lambda i: (0, 0)),
           pl.BlockSpec((D_MODEL, D_MODEL), lambda i: (0, 0))],
        out_specs=pl.BlockSpec((tm, D_MODEL), lambda i: (i, 0)),
        out_shape=jax.ShapeDtypeStruct((T, D_MODEL), F32),
        compiler_params=_cparams(("parallel",)),
        name="outproj",
    )(*ys, x2, gn.reshape(1, D_MODEL), w_out_b)


def _ffn_kernel(x_ref, g_ref, wgu_ref, wd_ref, fn_ref, o_ref, *, final_norm):
    x = x_ref[...]
    h = (_rms(x, D_MODEL) * g_ref[...]).astype(BF16)
    acc = x
    for c in range(D_FF // FF_CHUNK):
        lo = c * FF_CHUNK
        gate = _dot(h, wgu_ref[:, lo:lo + FF_CHUNK])
        up = _dot(h, wgu_ref[:, D_FF + lo:D_FF + lo + FF_CHUNK])
        acc = acc + _dot((jax.nn.silu(gate) * up).astype(BF16), wd_ref[lo:lo + FF_CHUNK, :])
    if final_norm:
        acc = _rms(acc, D_MODEL) * fn_ref[...]
    o_ref[...] = acc


def _ffn(x2, gain, wgu_b, wd_b, fnorm, final_norm, tm=256):
    T = x2.shape[0]
    row = pl.BlockSpec((1, D_MODEL), lambda i: (0, 0))
    return pl.pallas_call(
        functools.partial(_ffn_kernel, final_norm=final_norm),
        grid=(T // tm,),
        in_specs=[pl.BlockSpec((tm, D_MODEL), lambda i: (i, 0)), row,
                  pl.BlockSpec((D_MODEL, 2 * D_FF), lambda i: (0, 0)),
                  pl.BlockSpec((D_FF, D_MODEL), lambda i: (0, 0)), row],
        out_specs=pl.BlockSpec((tm, D_MODEL), lambda i: (i, 0)),
        out_shape=jax.ShapeDtypeStruct((T, D_MODEL), F32),
        compiler_params=_cparams(("parallel",)),
        name="ffn",
    )(x2, gain.reshape(1, D_MODEL), wgu_b, wd_b, fnorm.reshape(1, D_MODEL))


def _gather_cols(w, layout):
    parts = []
    for name, width in layout:
        if name is None:
            parts.append(jnp.zeros((w.shape[0], width), w.dtype))
        else:
            o, n = _OFF[name]
            assert n == width
            parts.append(w[:, o:o + n])
    return jnp.concatenate(parts, axis=1)


def _block_diag(w):
    h, d, _ = w.shape
    out = jnp.zeros((h * d, h * d), w.dtype)
    for i in range(h):
        out = out.at[i * d:(i + 1) * d, i * d:(i + 1) * d].set(w[i])
    return out


def _pad_to(a, shape):
    return jnp.pad(a, [(0, s - d) for s, d in zip(shape, a.shape)])


def _mla_weights(w_uq, w_ukv):
    H = MLA_HEADS
    wq = w_uq.reshape(MLA_Q_RANK, H, MLA_QK_DIM)
    wq = _pad_to(wq, (256, H, LANES)).reshape(256, H * LANES)
    wkv = w_ukv.reshape(MLA_KV_RANK, H, MLA_NOPE_DIM + MLA_V_DIM)
    wk = _pad_to(wkv[:, :, :MLA_NOPE_DIM], (MLA_KV_RANK, H, LANES)).reshape(MLA_KV_RANK, H * LANES)
    wv = _pad_to(wkv[:, :, MLA_NOPE_DIM:], (MLA_KV_RANK, H, LANES)).reshape(MLA_KV_RANK, H * LANES)
    return wq.astype(BF16), jnp.concatenate([wk, wv], axis=1).astype(BF16)


def _cmp_weights(cmp_pos, cmp_w1, cmp_b1, cmp_w2):
    half = CMP_LEN // 2
    Dh = NSA_HEAD_DIM
    w1 = cmp_w1.reshape(2, CMP_LEN, Dh, CMP_HIDDEN)

    def rows(part):
        wk = w1[0, part * half:(part + 1) * half]
        wv = w1[1, part * half:(part + 1) * half]
        z = jnp.zeros_like(wk)
        top = jnp.concatenate([wk, z], axis=-1)
        bot = jnp.concatenate([z, wv], axis=-1)
        return jnp.concatenate([top, bot], axis=1).reshape(half * 2 * Dh, 2 * CMP_HIDDEN)

    def pos_row(part):
        p = jnp.concatenate([cmp_pos[0, part * half:(part + 1) * half], cmp_pos[1, part * half:(part + 1) * half]],
                            axis=-1)
        return p.reshape(1, half * 2 * Dh)

    b1 = jnp.concatenate([cmp_b1[0], cmp_b1[1]]).reshape(1, 2 * CMP_HIDDEN)
    w2k = _pad_to(cmp_w2[0], (CMP_HIDDEN, LANES)).astype(BF16)
    w2v = _pad_to(cmp_w2[1], (CMP_HIDDEN, LANES)).astype(BF16)
    return pos_row(0), pos_row(1), rows(0).astype(BF16), rows(1).astype(BF16), b1, w2k, w2v


def kernel(x, positions, norm_mix, w_in, conv_w, conv_b, lru_wa, lru_ba, lru_wx, lru_bx, lru_lambda, cmp_pos, cmp_w1, cmp_b1, cmp_w2, gla_wg2, gla_bg2, gla_norm, mla_q_norm, mla_kv_norm, mla_w_uq, mla_w_ukv, group_norm, w_out, norm_ffn, w_gate_up, w_down, final_norm):
    B, S, D = x.shape
    assert D == D_MODEL and S % (CMP_STRIDE * 8) == 0 and S // SEL_LEN <= MAX_SEL_BLOCKS and S >= WINDOW + NSA_TQ
    depth = w_in.shape[0]
    T = B * S
    pos3 = positions.reshape(B, S, 1)
    ncmp = S // CMP_STRIDE
    ovl = _nsa_overlap(ncmp)
    x2 = x.reshape(T, D)
    for l in range(depth):
        w_p = jnp.concatenate([_gather_cols(w_in[l], lay) for lay in (_LAYOUT_A, _LAYOUT_B, _LAYOUT_C, _LAYOUT_D)],
                              axis=1).astype(BF16)
        pa, pb, pc, pd = _inproj(x2, norm_mix[l], w_p)
        pa, pb, pc, pd = (p.reshape(B, S, -1) for p in (pa, pb, pc, pd))

        wg = jnp.concatenate([_block_diag(lru_wa[l]), _block_diag(lru_wx[l])], axis=1).astype(BF16)
        bg = jnp.concatenate([lru_ba[l], lru_bx[l]]).reshape(1, 2 * GROUP_WIDTH)
        y_a = _rglru(pa, conv_w[l], conv_b[l], wg, bg, lru_lambda[l])

        q_n, kst, vs, kw, vw, kcvc = _nsa_proj(pb, pos3)
        t16 = kcvc.reshape(B, ncmp, CMP_STRIDE * 2 * NSA_HEAD_DIM)
        kc, vc = _nsa_compress(t16, *_cmp_weights(cmp_pos[l], cmp_w1[l], cmp_b1[l], cmp_w2[l]))
        y_b = _nsa_attn(q_n, pb, kc, vc, ovl, kst, vs, kw, vw)

        wg2_p = _pad_to(gla_wg2[l], (LANES, GLA_HEADS * GLA_DK))
        y_c = _gla(pc, wg2_p, gla_bg2[l], jnp.tile(gla_norm[l], GLA_HEADS))

        wq_p, wkv_p = _mla_weights(mla_w_uq[l], mla_w_ukv[l])
        q_m, kt_m, v_m = _mla_proj(pd, pos3, _pad_to(mla_q_norm[l], (256,)).reshape(1, 256), mla_kv_norm[l],
                                   wq_p, wkv_p)
        y_d = _mla_attn(q_m, kt_m, v_m)

        ys = [y.reshape(T, GROUP_WIDTH) for y in (y_a, y_b, y_c, y_d)]
        x2 = _outproj(ys, x2, group_norm[l], w_out[l].astype(BF16))
        x2 = _ffn(x2, norm_ffn[l], w_gate_up[l].astype(BF16), w_down[l].astype(BF16), final_norm,
                  final_norm=(l == depth - 1))
    return x2.reshape(B, S, D)
```

```python
import functools

import numpy as np
import jax
import jax.numpy as jnp
from jax import lax
from jax.experimental import pallas as pl
from jax.experimental.pallas import tpu as pltpu

F32 = jnp.float32
BF16 = jnp.bfloat16
HIGHEST = lax.Precision.HIGHEST

D_MODEL = 1024
GROUP_WIDTH = 256
ROPE_THETA = 10000.0
NORM_EPS = 1e-6
NEG_INF = -1e30
FORCE_SCORE = 1e9

LRU_C = 8.0
CONV_WIDTH = 4

NSA_HEADS = 4
NSA_HEAD_DIM = 64
CMP_LEN = 32
CMP_STRIDE = 16
CMP_HIDDEN = 256
SEL_LEN = 64
SEL_TOPN = 16
WINDOW = 512
NSA_TQ = 128
NSA_TK = 512
NSA_CHAINS = 2
MAX_SEL_BLOCKS = 128

GLA_HEADS = 4
GLA_DV = 64
GLA_DK = 32
GLA_GATE_RANK = 16
GLA_TAU = 16.0
GLA_TILE = 128
GLA_SUB = 16

MLA_HEADS = 4
MLA_V_DIM = 64
MLA_NOPE_DIM = 64
MLA_ROPE_DIM = 32
MLA_QK_DIM = 96
MLA_Q_RANK = 192
MLA_KV_RANK = 128
MLA_TQ = 512
MLA_TK = 512

D_FF = 2816
FF_CHUNK = 256

LANES = 128
V_ONES = 64
LOG2E = 1.4426950408889634
VMEM_LIMIT = 56 * 1024 * 1024

_OFF = {}
_o = 0
for _n, _w in (("a_x", 256), ("a_gate", 256), ("b_q", 256), ("k_c", 64), ("v_c", 64), ("k_s", 64), ("v_s", 64),
               ("k_w", 64), ("v_w", 64), ("b_gate", 12), ("c_q", 128), ("c_k", 128), ("c_v", 256), ("c_glr", 16),
               ("c_og", 256), ("d_cq", 192), ("d_ckv", 128), ("d_kr", 32)):
    _OFF[_n] = (_o, _w)
    _o += _w
IN_COLS = _o
_LAYOUT_A = (("a_x", 256), ("a_gate", 256))
_LAYOUT_B = (("b_q", 256), ("k_s", 64), ("k_c", 64), ("k_w", 64), (None, 64),
             ("v_c", 64), ("v_s", 64), ("v_w", 64), ("b_gate", 12), (None, 52))
_LAYOUT_C = (("c_q", 128), ("c_k", 128), ("c_v", 256), ("c_og", 256), ("c_glr", 16), (None, 112))
_LAYOUT_D = (("d_cq", 192), (None, 64), ("d_ckv", 128), (None, 64), ("d_kr", 32), (None, 32))
WA, WB, WC, WD = 512, 768, 896, 512


def _cparams(sem):
    return pltpu.CompilerParams(dimension_semantics=sem, vmem_limit_bytes=VMEM_LIMIT)


def _dot(a, b):
    return jnp.dot(a, b, preferred_element_type=F32)


def _dot_nt(a, b):
    return lax.dot_general(a, b, (((1,), (1,)), ((), ())), preferred_element_type=F32)


def _dot_tn(a, b):
    return lax.dot_general(a, b, (((0,), (0,)), ((), ())), preferred_element_type=F32)


def _dot_exact(a, b):
    return jnp.dot(a, b, precision=HIGHEST, preferred_element_type=F32)


def _rms(x, width):
    return x * lax.rsqrt(jnp.sum(x * x, axis=-1, keepdims=True) / width + NORM_EPS)


def _inproj_kernel(x_ref, g_ref, w_ref, oa_ref, ob_ref, oc_ref, od_ref):
    h = (_rms(x_ref[...], D_MODEL) * g_ref[...]).astype(BF16)
    off = 0
    for o_ref in (oa_ref, ob_ref, oc_ref, od_ref):
        w = o_ref.shape[1]
        o_ref[...] = _dot(h, w_ref[:, off:off + w])
        off += w


def _inproj(x2, gain, w_p, tm=512):
    T = x2.shape[0]
    ntot = w_p.shape[1]
    return pl.pallas_call(
        _inproj_kernel,
        grid=(T // tm,),
        in_specs=[pl.BlockSpec((tm, D_MODEL), lambda i: (i, 0)),
                  pl.BlockSpec((1, D_MODEL), lambda i: (0, 0)),
                  pl.BlockSpec((D_MODEL, ntot), lambda i: (0, 0))],
        out_specs=[pl.BlockSpec((tm, w), lambda i: (i, 0)) for w in (WA, WB, WC, WD)],
        out_shape=[jax.ShapeDtypeStruct((T, w), F32) for w in (WA, WB, WC, WD)],
        compiler_params=_cparams(("parallel",)),
        name="inproj",
    )(x2, gain.reshape(1, D_MODEL), w_p)


def _rglru_kernel(pa_ref, cw_ref, cb_ref, wg_ref, bg_ref, lam_ref, o_ref, xbuf, a_s, u_s, h_s, hlast):
    ts = pa_ref.shape[1]
    W = GROUP_WIDTH

    @pl.when(pl.program_id(1) == 0)
    def _():
        xbuf[0:8, :] = jnp.zeros((8, W), F32)
        hlast[...] = jnp.zeros_like(hlast)

    xbuf[8:8 + ts, :] = pa_ref[0, :, 0:W]
    xc = cb_ref[...]
    for k in range(CONV_WIDTH):
        lo = 8 - (CONV_WIDTH - 1) + k
        xc = xc + cw_ref[k:k + 1, :] * xbuf[lo:lo + ts, :]
    xbuf[0:8, :] = xbuf[ts:ts + 8, :]
    gates = _dot(xc.astype(BF16), wg_ref[...]) + bg_ref[...]
    r = jax.nn.sigmoid(gates[:, 0:W])
    i = jax.nn.sigmoid(gates[:, W:2 * W])
    log_a = (-LRU_C) * r * jax.nn.softplus(-lam_ref[...])
    a = jnp.exp(log_a)
    u = jnp.sqrt(-jnp.tanh(log_a) * (a * a + 1.0)) * (i * xc)
    a_s[...] = a
    u_s[...] = u

    def body(t, h):
        h = a_s[pl.ds(t, 1), :] * h + u_s[pl.ds(t, 1), :]
        h_s[pl.ds(t, 1), :] = h
        return h

    hlast[...] = lax.fori_loop(0, ts, body, hlast[...], unroll=8)
    o_ref[0] = h_s[...] * jax.nn.gelu(pa_ref[0, :, W:2 * W])


def _rglru(pa, conv_w, conv_b, wg, bg, lam, ts=512):
    B, S, _ = pa.shape
    W = GROUP_WIDTH
    full = lambda shape: pl.BlockSpec(shape, lambda b, j: (0,) * len(shape))
    return pl.pallas_call(
        _rglru_kernel,
        grid=(B, S // ts),
        in_specs=[pl.BlockSpec((1, ts, WA), lambda b, j: (b, j, 0)),
                  full((CONV_WIDTH, W)), full((1, W)), full((W, 2 * W)), full((1, 2 * W)), full((1, W))],
        out_specs=pl.BlockSpec((1, ts, W), lambda b, j: (b, j, 0)),
        out_shape=jax.ShapeDtypeStruct((B, S, W), F32),
        scratch_shapes=[pltpu.VMEM((ts + 8, W), F32), pltpu.VMEM((ts, W), F32), pltpu.VMEM((ts, W), F32),
                        pltpu.VMEM((ts, W), F32), pltpu.VMEM((1, W), F32)],
        compiler_params=_cparams(("parallel", "arbitrary")),
        name="rglru",
    )(pa, conv_w, conv_b.reshape(1, W), wg, bg, lam.reshape(1, W))


def _gla_kernel(pc_ref, wg2_ref, bg2_ref, gn_ref, lcum_ref, lsum_ref, hsum_ref, gmean_ref, smask_ref, o_ref, st_ref):
    TT, SUB = GLA_TILE, GLA_SUB
    NB = TT // SUB
    KW = GLA_HEADS * GLA_DK
    VW = GLA_HEADS * GLA_DV

    @pl.when(pl.program_id(1) == 0)
    def _():
        st_ref[...] = jnp.zeros_like(st_ref)

    q = pc_ref[0, :, 0:KW] * (GLA_DK ** -0.5)
    k = pc_ref[0, :, KW:2 * KW]
    v = pc_ref[0, :, 2 * KW:2 * KW + VW]
    og = pc_ref[0, :, 2 * KW + VW:2 * KW + 2 * VW]
    glr = pc_ref[0, :, 2 * KW + 2 * VW:2 * KW + 2 * VW + LANES]
    log_a = jax.nn.log_sigmoid(_dot_exact(glr, wg2_ref[...]) + bg2_ref[...]) * (1.0 / GLA_TAU)
    b = _dot_exact(lcum_ref[...], log_a)
    bl = _dot_exact(lsum_ref[...], log_a)

    q3 = q.reshape(NB, SUB, KW)
    k3 = k.reshape(NB, SUB, KW)
    b3 = b.reshape(NB, SUB, KW)
    v3 = v.reshape(NB, SUB, VW)
    row = lax.broadcasted_iota(jnp.int32, (NB, SUB, KW), 1)
    terms = []
    for j in range(SUB):
        kj = jnp.broadcast_to(k3[:, j:j + 1, :], (NB, SUB, KW))
        bj = jnp.broadcast_to(b3[:, j:j + 1, :], (NB, SUB, KW))
        e = q3 * kj * jnp.exp(jnp.where(row >= j, b3 - bj, NEG_INF))
        terms.append(e.reshape(TT, KW))
    e_all = jnp.concatenate(terms, axis=0)
    e_hi = e_all.astype(BF16)
    e_lo = (e_all - e_hi.astype(F32)).astype(BF16)
    attn = _dot(e_hi, hsum_ref[...]) + _dot(e_lo, hsum_ref[...])
    o = jnp.zeros((NB, SUB, VW), F32)
    for j in range(SUB):
        vj = jnp.broadcast_to(v3[:, j:j + 1, :], (NB, SUB, VW))
        o = o + attn[j * TT:(j + 1) * TT, :].reshape(NB, SUB, VW) * vj
    o = o.reshape(TT, VW)

    qd = (q * jnp.exp(b)).astype(BF16)
    kd = (k * jnp.exp(bl - b)).astype(BF16)
    dec = jnp.exp(bl)
    vb = v.astype(BF16)
    st = st_ref[...]
    inter = []
    for n in range(NB):
        rows = slice(n * SUB, (n + 1) * SUB)
        inter.append(_dot_nt(qd[rows], st.astype(BF16)))
        st = st * dec[n * SUB:n * SUB + 1, :] + _dot_tn(vb[rows], kd[rows]) * smask_ref[...]
    st_ref[...] = st
    o = o + jnp.concatenate(inter, axis=0)

    ms = _dot_exact(o * o, gmean_ref[...])
    o_ref[0] = o * lax.rsqrt(ms + NORM_EPS) * gn_ref[...] * jax.nn.silu(og)


def _gla_consts():
    TT, SUB = GLA_TILE, GLA_SUB
    KW, VW = GLA_HEADS * GLA_DK, GLA_HEADS * GLA_DV
    i = np.arange(TT)
    same = (i[:, None] // SUB) == (i[None, :] // SUB)
    lcum = (same & (i[None, :] <= i[:, None])).astype(np.float32)
    lsum = same.astype(np.float32)
    hk = np.arange(KW) // GLA_DK
    hv = np.arange(VW) // GLA_DV
    hsum = (hk[:, None] == hv[None, :]).astype(np.float32)
    gmean = (hv[:, None] == hv[None, :]).astype(np.float32) / GLA_DV
    smask = (hv[:, None] == hk[None, :]).astype(np.float32)
    return (jnp.asarray(lcum), jnp.asarray(lsum), jnp.asarray(hsum, dtype=BF16), jnp.asarray(gmean),
            jnp.asarray(smask))


def _gla(pc, wg2_p, bg2, gn_t):
    B, S, _ = pc.shape
    TT = GLA_TILE
    KW, VW = GLA_HEADS * GLA_DK, GLA_HEADS * GLA_DV
    lcum, lsum, hsum, gmean, smask = _gla_consts()
    full = lambda shape: pl.BlockSpec(shape, lambda b, j: (0,) * len(shape))
    return pl.pallas_call(
        _gla_kernel,
        grid=(B, S // TT),
        in_specs=[pl.BlockSpec((1, TT, WC), lambda b, j: (b, j, 0)),
                  full((LANES, KW)), full((1, KW)), full((1, VW)), full((TT, TT)), full((TT, TT)),
                  full((KW, VW)), full((VW, VW)), full((VW, KW))],
        out_specs=pl.BlockSpec((1, TT, VW), lambda b, j: (b, j, 0)),
        out_shape=jax.ShapeDtypeStruct((B, S, VW), F32),
        scratch_shapes=[pltpu.VMEM((VW, KW), F32)],
        compiler_params=_cparams(("parallel", "arbitrary")),
        name="gla",
    )(pc, wg2_p, bg2.reshape(1, KW), gn_t.reshape(1, VW), lcum, lsum, hsum, gmean, smask)


def _rope_lanes(x, cos, sin_signed, half):
    lane = lax.broadcasted_iota(jnp.int32, cos.shape, 1)
    lo = (lane % (2 * half)) < half
    outs = []
    for c in range(x.shape[1] // LANES):
        xs = x[:, c * LANES:(c + 1) * LANES]
        rot = jnp.where(lo, pltpu.roll(xs, LANES - half, 1), pltpu.roll(xs, half, 1))
        outs.append(xs * cos + rot * sin_signed)
    return outs[0] if len(outs) == 1 else jnp.concatenate(outs, axis=1)


def _rope_table_kernel(pos_ref, invf_ref, cos_ref, sin_ref, *, half):
    ang = pos_ref[0].astype(F32) * invf_ref[...]
    lane = lax.broadcasted_iota(jnp.int32, ang.shape, 1)
    sin = jnp.sin(ang)
    cos_ref[0] = jnp.cos(ang)
    sin_ref[0] = jnp.where((lane % (2 * half)) < half, -sin, sin)


def _rope_tables(pos3, d, lead_zeros=0, tail_zeros=0, tm=1024):
    B, S, _ = pos3.shape
    inv = ROPE_THETA ** (-jnp.arange(0, d, 2, dtype=F32) / d)
    reps = (LANES - lead_zeros - tail_zeros) // d
    invf = jnp.concatenate([jnp.zeros((lead_zeros,), F32), jnp.tile(jnp.concatenate([inv, inv]), reps),
                            jnp.zeros((tail_zeros,), F32)]).reshape(1, LANES)
    spec = pl.BlockSpec((1, tm, LANES), lambda b, j: (b, j, 0))
    shape = jax.ShapeDtypeStruct((B, S, LANES), F32)
    return pl.pallas_call(
        functools.partial(_rope_table_kernel, half=d // 2),
        grid=(B, S // tm),
        in_specs=[pl.BlockSpec((1, tm, 1), lambda b, j: (b, j, 0)), pl.BlockSpec((1, LANES), lambda b, j: (0, 0))],
        out_specs=[spec, spec],
        out_shape=[shape, shape],
        compiler_params=_cparams(("parallel", "parallel")),
        name="rope_tables",
    )(pos3, invf)


def _mla_proj_kernel(pd_ref, cos_ref, sin_ref, qn_ref, kvn_ref, wq_ref, wkv_ref, qt_ref, k_ref, vt_ref):
    H = MLA_HEADS
    tm = pd_ref.shape[1]
    cos, sin = cos_ref[0], sin_ref[0]
    cq = (_rms(pd_ref[0, :, 0:256], MLA_Q_RANK) * qn_ref[...]).astype(BF16)
    ckv = (_rms(pd_ref[0, :, 256:384], MLA_KV_RANK) * kvn_ref[...]).astype(BF16)
    kr = _rope_lanes(pd_ref[0, :, 384:512], cos, sin, MLA_ROPE_DIM // 2)
    q_all = _dot(cq, wq_ref[...])
    kv_all = _dot(ckv, wkv_ref[...])
    low = lax.broadcasted_iota(jnp.int32, (tm, LANES), 1) < MLA_V_DIM
    for h in range(H):
        qh = _rope_lanes(q_all[:, h * LANES:(h + 1) * LANES], cos, sin, MLA_ROPE_DIM // 2)
        qt_ref[0, h, 0] = (qh * (MLA_QK_DIM ** -0.5 * LOG2E)).T.astype(BF16)
        k_ref[0, h] = (kv_all[:, h * LANES:(h + 1) * LANES] + kr).astype(BF16)
        vt_ref[0, h, 0] = jnp.where(low, kv_all[:, (H + h) * LANES:(H + h + 1) * LANES], 1.0).T.astype(BF16)


def _mla_proj(pd, cos, sin, qn_p, kvn, wq_p, wkv_p):
    B, S, _ = pd.shape
    H = MLA_HEADS
    tm = MLA_TK
    assert MLA_TQ == tm
    full = lambda shape: pl.BlockSpec(shape, lambda b, j: (0,) * len(shape))
    tspec = pl.BlockSpec((1, H, 1, LANES, tm), lambda b, j: (b, 0, j, 0, 0))
    tshape = jax.ShapeDtypeStruct((B, H, S // tm, LANES, tm), BF16)
    return pl.pallas_call(
        _mla_proj_kernel,
        grid=(B, S // tm),
        in_specs=[pl.BlockSpec((1, tm, WD), lambda b, j: (b, j, 0)),
                  pl.BlockSpec((1, tm, LANES), lambda b, j: (b, j, 0)),
                  pl.BlockSpec((1, tm, LANES), lambda b, j: (b, j, 0)),
                  full((1, 256)), full((1, LANES)), full((256, H * LANES)), full((LANES, 2 * H * LANES))],
        out_specs=[tspec, pl.BlockSpec((1, H, tm, LANES), lambda b, j: (b, 0, j, 0)), tspec],
        out_shape=[tshape, jax.ShapeDtypeStruct((B, H, S, LANES), BF16), tshape],
        compiler_params=_cparams(("parallel", "parallel")),
        name="mla_proj",
    )(pd, cos, sin, qn_p, kvn.reshape(1, LANES), wq_p, wkv_p)


def _softmax_update(s_refs, vts, m_refs, acc_refs, mask):
    ss = [r[...] for r in s_refs]
    if mask is not None:
        ss = [jnp.where(mask, s, NEG_INF) for s in ss]
    m_prevs = [r[...] for r in m_refs]
    m_news = [jnp.maximum(mp, jnp.max(s, axis=0, keepdims=True)) for mp, s in zip(m_prevs, ss)]
    ps = [jnp.exp2(s - mn).astype(BF16) for s, mn in zip(ss, m_news)]
    for vt1, m_ref, acc_ref, p, mp, mn in zip(vts, m_refs, acc_refs, ps, m_prevs, m_news):
        acc_ref[...] = jnp.exp2(mp - mn) * acc_ref[...] + _dot(vt1, p)
        m_ref[...] = mn


def _flash_loop(n_full, scores, values, mask, buf_a, buf_b, m_refs, acc_refs):
    for r in m_refs:
        r[...] = jnp.full(r.shape, NEG_INF, F32)
    for r in acc_refs:
        r[...] = jnp.zeros(r.shape, F32)

    def fill(bufs, j):
        for r, sc in zip(bufs, scores(j)):
            r[...] = sc

    fill(buf_a, 0)

    def body(jj, carry):
        j = 2 * jj
        fill(buf_b, j + 1)
        _softmax_update(buf_a, values(j), m_refs, acc_refs, None)
        fill(buf_a, j + 2)
        _softmax_update(buf_b, values(j + 1), m_refs, acc_refs, None)
        return carry

    lax.fori_loop(0, n_full // 2, body, 0)

    @pl.when(n_full % 2 == 1)
    def _():
        fill(buf_b, n_full)
        _softmax_update(buf_a, values(n_full - 1), m_refs, acc_refs, None)
        _softmax_update(buf_b, values(n_full), m_refs, acc_refs, mask)

    @pl.when(n_full % 2 == 0)
    def _():
        _softmax_update(buf_a, values(n_full), m_refs, acc_refs, mask)


def _normalize(acc):
    return acc / jnp.maximum(acc[V_ONES:V_ONES + 1, :], 1e-30)


def _mla_attn_kernel(qt_ref, k_ref, vt_ref, o_ref, *scratch):
    tq, tk = MLA_TQ, MLA_TK
    H = MLA_HEADS
    m_refs, acc_refs, buf_a, buf_b = (scratch[i * H:(i + 1) * H] for i in range(4))
    s0 = pl.program_id(1) * tq
    n_full = s0 // tk

    def scores(j):
        off = pl.multiple_of(j * tk, tk)
        return [_dot(k_ref[0, h, pl.ds(off, tk), :], qt_ref[0, h, 0]) for h in range(H)]

    def values(j):
        return [vt_ref[0, h, j] for h in range(H)]

    kpos = n_full * tk + lax.broadcasted_iota(jnp.int32, (tk, tq), 0)
    t = s0 + lax.broadcasted_iota(jnp.int32, (tk, tq), 1)
    _flash_loop(n_full, scores, values, kpos <= t, buf_a, buf_b, m_refs, acc_refs)
    o_t = jnp.concatenate([_normalize(acc_refs[h][...])[0:MLA_V_DIM, :] for h in range(H)], axis=0)
    o_ref[0] = o_t.T


def _mla_attn(qt, k, vt):
    B, H, S, _ = k.shape
    tq, tk = MLA_TQ, MLA_TK
    return pl.pallas_call(
        _mla_attn_kernel,
        grid=(B, S // tq),
        in_specs=[pl.BlockSpec((1, H, 1, LANES, tq), lambda b, i: (b, 0, i, 0, 0)),
                  pl.BlockSpec((1, H, S, LANES), lambda b, i: (b, 0, 0, 0)),
                  pl.BlockSpec((1, H, S // tk, LANES, tk), lambda b, i: (b, 0, 0, 0, 0))],
        out_specs=pl.BlockSpec((1, tq, H * MLA_V_DIM), lambda b, i: (b, i, 0)),
        out_shape=jax.ShapeDtypeStruct((B, S, H * MLA_V_DIM), F32),
        scratch_shapes=([pltpu.VMEM((1, tq), F32)] * H + [pltpu.VMEM((LANES, tq), F32)] * H
                        + [pltpu.VMEM((tk, tq), F32)] * (2 * H)),
        compiler_params=_cparams(("parallel", "arbitrary")),
        name="mla_attn",
    )(qt, k, vt)


def _nsa_proj_kernel(pb_ref, cos_ref, sin_ref, qt_ref, ksa_ref, vst_ref, kw_ref, vwt_ref, kcvc_ref):
    tm = pb_ref.shape[1]
    H, TQ = NSA_HEADS, NSA_TQ
    r = _rope_lanes(pb_ref[0, :, 0:512], cos_ref[0], sin_ref[0], NSA_HEAD_DIM // 2)
    nr0 = pb_ref[0, :, 512:640]
    nr1 = pb_ref[0, :, 640:768]
    lane = lax.broadcasted_iota(jnp.int32, (tm, LANES), 1)
    low = lane < NSA_HEAD_DIM
    scale = NSA_HEAD_DIM ** -0.5 * LOG2E
    for h in range(H):
        seg = r[:, (h // 2) * LANES:(h // 2 + 1) * LANES]
        if h % 2:
            seg = pltpu.roll(seg, NSA_HEAD_DIM, 1)
        qh_t = jnp.where(low, seg * scale, 0.0).T.astype(BF16)
        for c in range(tm // TQ):
            qt_ref[0, c, :, h * TQ:(h + 1) * TQ] = qh_t[:, c * TQ:(c + 1) * TQ]
    kseg = r[:, 256:384]
    blk = lax.shift_right_logical(lax.broadcasted_iota(jnp.int32, (tm, LANES), 0) + pl.program_id(1) * tm, 6)
    ksa_ref[0, :, 0:LANES] = jnp.where(low, kseg, 0.0).astype(BF16)
    ksa_ref[0, :, LANES:2 * LANES] = jnp.where(blk == lane, 1.0, 0.0).astype(BF16)
    kw_ref[0] = r[:, 384:512].astype(BF16)
    vst_ref[0, 0] = jnp.where(low, pltpu.roll(nr0, NSA_HEAD_DIM, 1), 1.0).T.astype(BF16)
    vw_t = jnp.where(low, nr1, 1.0).T.astype(BF16)
    for c in range(tm // LANES):
        vwt_ref[0, c] = vw_t[:, c * LANES:(c + 1) * LANES]
    kcvc_ref[0] = jnp.where(low, pltpu.roll(kseg, NSA_HEAD_DIM, 1), pltpu.roll(nr0, NSA_HEAD_DIM, 1))


def _nsa_proj(pb, cos, sin):
    B, S, _ = pb.shape
    H, TQ = NSA_HEADS, NSA_TQ
    tm = NSA_TK
    tok = lambda w: pl.BlockSpec((1, tm, w), lambda b, j: (b, j, 0))
    return pl.pallas_call(
        _nsa_proj_kernel,
        grid=(B, S // tm),
        in_specs=[tok(WB), tok(LANES), tok(LANES)],
        out_specs=[pl.BlockSpec((1, tm // TQ, LANES, H * TQ), lambda b, j: (b, j, 0, 0)),
                   tok(2 * LANES),
                   pl.BlockSpec((1, 1, LANES, tm), lambda b, j: (b, j, 0, 0)),
                   tok(LANES),
                   pl.BlockSpec((1, tm // LANES, LANES, LANES), lambda b, j: (b, j, 0, 0)),
                   tok(LANES)],
        out_shape=[jax.ShapeDtypeStruct((B, S // TQ, LANES, H * TQ), BF16),
                   jax.ShapeDtypeStruct((B, S, 2 * LANES), BF16),
                   jax.ShapeDtypeStruct((B, S // tm, LANES, tm), BF16),
                   jax.ShapeDtypeStruct((B, S, LANES), BF16),
                   jax.ShapeDtypeStruct((B, S // LANES, LANES, LANES), BF16),
                   jax.ShapeDtypeStruct((B, S, LANES), F32)],
        compiler_params=_cparams(("parallel", "parallel")),
        name="nsa_proj",
    )(pb, cos, sin)


def _nsa_cmp_kernel(t_ref, ptop_ref, pbot_ref, wtop_ref, wbot_ref, b1_ref, w2k_ref, w2v_ref, kc_ref, vct_ref, sh_ref):
    n = t_ref.shape[1]
    t = t_ref[0]
    top = _dot((t + ptop_ref[...]).astype(BF16), wtop_ref[...])
    sh_ref[0:n, :] = _dot((t + pbot_ref[...]).astype(BF16), wbot_ref[...])
    sh_ref[n:n + 8, :] = jnp.zeros((8, sh_ref.shape[1]), F32)
    hid = jax.nn.gelu(top + sh_ref[pl.ds(1, n), :] + b1_ref[...])
    kc_ref[0] = _dot(hid[:, 0:CMP_HIDDEN].astype(BF16), w2k_ref[...]).astype(BF16)
    vc = _dot(hid[:, CMP_HIDDEN:2 * CMP_HIDDEN].astype(BF16), w2v_ref[...])
    low = lax.broadcasted_iota(jnp.int32, vc.shape, 1) < NSA_HEAD_DIM
    vct_ref[0] = jnp.where(low, vc, 1.0).T.astype(BF16)


def _nsa_compress(t16, ptop, pbot, wtop, wbot, b1, w2k, w2v):
    B, n, F = t16.shape
    full = lambda shape: pl.BlockSpec(shape, lambda b: (0,) * len(shape))
    return pl.pallas_call(
        _nsa_cmp_kernel,
        grid=(B,),
        in_specs=[pl.BlockSpec((1, n, F), lambda b: (b, 0, 0)), full((1, F)), full((1, F)),
                  full((F, 2 * CMP_HIDDEN)), full((F, 2 * CMP_HIDDEN)), full((1, 2 * CMP_HIDDEN)),
                  full((CMP_HIDDEN, LANES)), full((CMP_HIDDEN, LANES))],
        out_specs=[pl.BlockSpec((1, n, LANES), lambda b: (b, 0, 0)), pl.BlockSpec((1, LANES, n), lambda b: (b, 0, 0))],
        out_shape=[jax.ShapeDtypeStruct((B, n, LANES), BF16), jax.ShapeDtypeStruct((B, LANES, n), BF16)],
        scratch_shapes=[pltpu.VMEM((n + 8, 2 * CMP_HIDDEN), F32)],
        compiler_params=_cparams(("parallel",)),
        name="nsa_compress",
    )(t16, ptop, pbot, wtop, wbot, b1, w2k, w2v)


def _masked_exp(s, mask):
    s = jnp.where(mask, s, NEG_INF)
    return jnp.where(mask, jnp.exp2(s - jnp.max(s, axis=0, keepdims=True)), 0.0)


def _nsa_attn_kernel(qt_ref, g_ref, kc_ref, vct_ref, ovl_ref, ksa_ref, vst_ref, kw_ref, vwt_ref, o_ref, *scratch):
    H, TQ, TK = NSA_HEADS, NSA_TQ, NSA_TK
    M = H * TQ
    NB = MAX_SEL_BLOCKS
    s0 = pl.program_id(1) * TQ
    ncmp = kc_ref.shape[1]
    qt = qt_ref[0, 0]
    t_of = lambda shape: s0 + lax.broadcasted_iota(jnp.int32, shape, 1) % TQ

    cend = lax.broadcasted_iota(jnp.int32, (ncmp, M), 0) * CMP_STRIDE + (CMP_LEN - 1)
    e_c = _masked_exp(_dot(kc_ref[0], qt), cend <= t_of((ncmp, M)))
    a_c = _dot(vct_ref[0], e_c.astype(BF16))
    inv_c = 1.0 / jnp.maximum(a_c[V_ONES:V_ONES + 1, :], 1e-30)
    o_c = a_c * inv_c
    p_c = e_c * inv_c

    psum = p_c[:, 0:TQ] + p_c[:, TQ:2 * TQ] + p_c[:, 2 * TQ:3 * TQ] + p_c[:, 3 * TQ:4 * TQ]
    p_hi = psum.astype(BF16)
    p_lo = (psum - p_hi.astype(F32)).astype(BF16)
    imp = _dot(ovl_ref[...], p_hi) + _dot(ovl_ref[...], p_lo)
    jb = lax.broadcasted_iota(jnp.int32, (NB, TQ), 0)
    tq_l = s0 + lax.broadcasted_iota(jnp.int32, (NB, TQ), 1)
    cur = lax.shift_right_logical(tq_l, 6)
    forced = (jb == 0) | (jb == cur) | (jb == cur - 1)
    score = jnp.where(forced, FORCE_SCORE, jnp.where(jb * SEL_LEN <= tq_l, imp, NEG_INF))
    jbf = jb.astype(F32)
    sel = jnp.zeros((NB, TQ), F32)
    for _ in range(SEL_TOPN):
        best = jnp.max(score, axis=0, keepdims=True)
        first = jnp.min(jnp.where(score == best, jbf, float(NB)), axis=0, keepdims=True)
        hit = jbf == first
        sel = jnp.where(hit, 1.0, sel)
        score = jnp.where(hit, -3e38, score)
    bias = ((sel - 1.0) * (-NEG_INF)).astype(BF16)
    q_aug = jnp.concatenate([qt, jnp.concatenate([bias] * H, axis=1)], axis=0)

    NC = NSA_CHAINS
    CW = M // NC
    m_refs, acc_refs, buf_a, buf_b = (scratch[i * NC:(i + 1) * NC] for i in range(4))
    n_full = s0 // TK
    q_cols = [q_aug[:, c * CW:(c + 1) * CW] for c in range(NC)]

    def scores(j):
        k = ksa_ref[0, pl.ds(pl.multiple_of(j * TK, TK), TK), :]
        return [_dot(k, qc) for qc in q_cols]

    def values(j):
        return [vst_ref[0, j]] * NC

    kpos = n_full * TK + lax.broadcasted_iota(jnp.int32, (TK, CW), 0)
    _flash_loop(n_full, scores, values, kpos <= t_of((TK, CW)), buf_a, buf_b, m_refs, acc_refs)
    o_s = jnp.concatenate([_normalize(acc_refs[c][...]) for c in range(NC)], axis=1)

    WK = WINDOW + TQ
    w0 = pl.multiple_of(jnp.maximum(s0 - WINDOW, 0), TQ)
    wpos = w0 + lax.broadcasted_iota(jnp.int32, (WK, M), 0)
    t_w = t_of((WK, M))
    e_w = _masked_exp(_dot(kw_ref[0, pl.ds(w0, WK), :], qt), (wpos <= t_w) & (wpos > t_w - WINDOW)).astype(BF16)
    a_w = _dot(vwt_ref[0, w0 // LANES], e_w[0:LANES])
    for c in range(1, WK // LANES):
        a_w = a_w + _dot(vwt_ref[0, w0 // LANES + c], e_w[c * LANES:(c + 1) * LANES])
    o_w = _normalize(a_w)

    g = jax.nn.sigmoid(g_ref[0]).T
    outs = []
    for h in range(H):
        cols = slice(h * TQ, (h + 1) * TQ)
        r0 = NSA_HEAD_DIM + 3 * h
        o_h = (g[r0:r0 + 1, :] * o_c[0:NSA_HEAD_DIM, cols] + g[r0 + 1:r0 + 2, :] * o_s[0:NSA_HEAD_DIM, cols]
               + g[r0 + 2:r0 + 3, :] * o_w[0:NSA_HEAD_DIM, cols])
        outs.append(o_h)
    o_ref[0] = jnp.concatenate(outs, axis=0).T


def _nsa_attn(qt, pb, kc, vct, ovl, ksa, vst, kw, vwt):
    B, S, _ = ksa.shape
    H, TQ, TK = NSA_HEADS, NSA_TQ, NSA_TK
    ncmp = kc.shape[1]
    per_b = lambda *shape: pl.BlockSpec((1,) + shape, lambda b, i: (b,) + (0,) * len(shape))
    return pl.pallas_call(
        _nsa_attn_kernel,
        grid=(B, S // TQ),
        in_specs=[pl.BlockSpec((1, 1, LANES, H * TQ), lambda b, i: (b, i, 0, 0)),
                  pl.BlockSpec((1, TQ, LANES), lambda b, i: (b, i, WB // LANES - 1)),
                  per_b(ncmp, LANES), per_b(LANES, ncmp),
                  pl.BlockSpec((MAX_SEL_BLOCKS, ncmp), lambda b, i: (0, 0)),
                  per_b(S, 2 * LANES), per_b(S // TK, LANES, TK), per_b(S, LANES), per_b(S // LANES, LANES, LANES)],
        out_specs=pl.BlockSpec((1, TQ, H * NSA_HEAD_DIM), lambda b, i: (b, i, 0)),
        out_shape=jax.ShapeDtypeStruct((B, S, H * NSA_HEAD_DIM), F32),
        scratch_shapes=([pltpu.VMEM((1, H * TQ // NSA_CHAINS), F32)] * NSA_CHAINS
                        + [pltpu.VMEM((LANES, H * TQ // NSA_CHAINS), F32)] * NSA_CHAINS
                        + [pltpu.VMEM((TK, H * TQ // NSA_CHAINS), F32)] * (2 * NSA_CHAINS)),
        compiler_params=_cparams(("parallel", "arbitrary")),
        name="nsa_attn",
    )(qt, pb, kc, vct, ovl, ksa, vst, kw, vwt)


def _nsa_overlap(ncmp):
    n = np.arange(ncmp)[None, :] * CMP_STRIDE
    s = np.arange(MAX_SEL_BLOCKS)[:, None] * SEL_LEN
    real = np.arange(ncmp)[None, :] < ncmp - 1
    ovl = (n < s + SEL_LEN) & (n + CMP_LEN > s) & real
    return jnp.asarray(ovl.astype(np.float32), dtype=BF16)


def _outproj_kernel(ya_ref, yb_ref, yc_ref, yd_ref, x_ref, gn_ref, w_ref, o_ref):
    acc = x_ref[...]
    for g, y_ref in enumerate((ya_ref, yb_ref, yc_ref, yd_ref)):
        sl = slice(g * GROUP_WIDTH, (g + 1) * GROUP_WIDTH)
        y = (_rms(y_ref[...], GROUP_WIDTH) * gn_ref[:, sl]).astype(BF16)
        acc = acc + _dot(y, w_ref[sl, :])
    o_ref[...] = acc


def _outproj(ys, x2, gn, w_out_b, tm=512):
    T = x2.shape[0]
    W = GROUP_WIDTH
    return pl.pallas_call(
        _outproj_kernel,
        grid=(T // tm,),
        in_specs=[pl.BlockSpec((tm, W), lambda i: (i, 0))] * 4
        + [pl.BlockSpec((tm, D_MODEL), lambda i: (i, 0)),
           pl.BlockSpec((1, D_MODEL), lambda i: (0, 0)),
           pl.BlockSpec((D_MODEL, D_MODEL), lambda i: (0, 0))],
        out_specs=pl.BlockSpec((tm, D_MODEL), lambda i: (i, 0)),
        out_shape=jax.ShapeDtypeStruct((T, D_MODEL), F32),
        compiler_params=_cparams(("parallel",)),
        name="outproj",
    )(*ys, x2, gn.reshape(1, D_MODEL), w_out_b)


def _ffn_kernel(x_ref, g_ref, wgu_ref, wd_ref, fn_ref, o_ref, *, final_norm):
    x = x_ref[...]
    h = (_rms(x, D_MODEL) * g_ref[...]).astype(BF16)
    acc = x
    for c in range(D_FF // FF_CHUNK):
        lo = c * FF_CHUNK
        gate = _dot(h, wgu_ref[:, lo:lo + FF_CHUNK])
        up = _dot(h, wgu_ref[:, D_FF + lo:D_FF + lo + FF_CHUNK])
        acc = acc + _dot((jax.nn.silu(gate) * up).astype(BF16), wd_ref[lo:lo + FF_CHUNK, :])
    if final_norm:
        acc = _rms(acc, D_MODEL) * fn_ref[...]
    o_ref[...] = acc


def _ffn(x2, gain, wgu_b, wd_b, fnorm, final_norm, tm=256):
    T = x2.shape[0]
    row = pl.BlockSpec((1, D_MODEL), lambda i: (0, 0))
    return pl.pallas_call(
        functools.partial(_ffn_kernel, final_norm=final_norm),
        grid=(T // tm,),
        in_specs=[pl.BlockSpec((tm, D_MODEL), lambda i: (i, 0)), row,
                  pl.BlockSpec((D_MODEL, 2 * D_FF), lambda i: (0, 0)),
                  pl.BlockSpec((D_FF, D_MODEL), lambda i: (0, 0)), row],
        out_specs=pl.BlockSpec((tm, D_MODEL), lambda i: (i, 0)),
        out_shape=jax.ShapeDtypeStruct((T, D_MODEL), F32),
        compiler_params=_cparams(("parallel",)),
        name="ffn",
    )(x2, gain.reshape(1, D_MODEL), wgu_b, wd_b, fnorm.reshape(1, D_MODEL))


def _gather_cols(w, layout):
    parts = []
    for name, width in layout:
        if name is None:
            parts.append(jnp.zeros((w.shape[0], width), w.dtype))
        else:
            o, n = _OFF[name]
            assert n == width
            parts.append(w[:, o:o + n])
    return jnp.concatenate(parts, axis=1)


def _block_diag(w):
    h, d, _ = w.shape
    out = jnp.zeros((h * d, h * d), w.dtype)
    for i in range(h):
        out = out.at[i * d:(i + 1) * d, i * d:(i + 1) * d].set(w[i])
    return out


def _pad_to(a, shape):
    return jnp.pad(a, [(0, s - d) for s, d in zip(shape, a.shape)])


def _mla_weights(w_uq, w_ukv):
    H = MLA_HEADS
    wq = w_uq.reshape(MLA_Q_RANK, H, MLA_QK_DIM)
    wq = _pad_to(wq, (256, H, LANES)).reshape(256, H * LANES)
    wkv = w_ukv.reshape(MLA_KV_RANK, H, MLA_NOPE_DIM + MLA_V_DIM)
    wk = _pad_to(wkv[:, :, :MLA_NOPE_DIM], (MLA_KV_RANK, H, LANES)).reshape(MLA_KV_RANK, H * LANES)
    wv = _pad_to(wkv[:, :, MLA_NOPE_DIM:], (MLA_KV_RANK, H, LANES)).reshape(MLA_KV_RANK, H * LANES)
    return wq.astype(BF16), jnp.concatenate([wk, wv], axis=1).astype(BF16)


def _cmp_weights(cmp_pos, cmp_w1, cmp_b1, cmp_w2):
    half = CMP_LEN // 2
    Dh = NSA_HEAD_DIM
    w1 = cmp_w1.reshape(2, CMP_LEN, Dh, CMP_HIDDEN)

    def rows(part):
        wk = w1[0, part * half:(part + 1) * half]
        wv = w1[1, part * half:(part + 1) * half]
        z = jnp.zeros_like(wk)
        top = jnp.concatenate([wk, z], axis=-1)
        bot = jnp.concatenate([z, wv], axis=-1)
        return jnp.concatenate([top, bot], axis=1).reshape(half * 2 * Dh, 2 * CMP_HIDDEN)

    def pos_row(part):
        p = jnp.concatenate([cmp_pos[0, part * half:(part + 1) * half], cmp_pos[1, part * half:(part + 1) * half]],
                            axis=-1)
        return p.reshape(1, half * 2 * Dh)

    b1 = jnp.concatenate([cmp_b1[0], cmp_b1[1]]).reshape(1, 2 * CMP_HIDDEN)
    w2k = _pad_to(cmp_w2[0], (CMP_HIDDEN, LANES)).astype(BF16)
    w2v = _pad_to(cmp_w2[1], (CMP_HIDDEN, LANES)).astype(BF16)
    return pos_row(0), pos_row(1), rows(0).astype(BF16), rows(1).astype(BF16), b1, w2k, w2v


def kernel(x, positions, norm_mix, w_in, conv_w, conv_b, lru_wa, lru_ba, lru_wx, lru_bx, lru_lambda, cmp_pos, cmp_w1, cmp_b1, cmp_w2, gla_wg2, gla_bg2, gla_norm, mla_q_norm, mla_kv_norm, mla_w_uq, mla_w_ukv, group_norm, w_out, norm_ffn, w_gate_up, w_down, final_norm):
    B, S, D = x.shape
    assert D == D_MODEL and S % (CMP_STRIDE * 8) == 0 and S // SEL_LEN <= MAX_SEL_BLOCKS and S >= WINDOW + NSA_TQ
    depth = w_in.shape[0]
    T = B * S
    pos3 = positions.reshape(B, S, 1)
    ncmp = S // CMP_STRIDE
    ovl = _nsa_overlap(ncmp)
    cos_n, sin_n = _rope_tables(pos3, NSA_HEAD_DIM)
    cos_m, sin_m = _rope_tables(pos3, MLA_ROPE_DIM, lead_zeros=MLA_NOPE_DIM, tail_zeros=LANES - MLA_QK_DIM)
    x2 = x.reshape(T, D)
    for l in range(depth):
        w_p = jnp.concatenate([_gather_cols(w_in[l], lay) for lay in (_LAYOUT_A, _LAYOUT_B, _LAYOUT_C, _LAYOUT_D)],
                              axis=1).astype(BF16)
        pa, pb, pc, pd = _inproj(x2, norm_mix[l], w_p)
        pa, pb, pc, pd = (p.reshape(B, S, -1) for p in (pa, pb, pc, pd))

        wg = jnp.concatenate([_block_diag(lru_wa[l]), _block_diag(lru_wx[l])], axis=1).astype(BF16)
        bg = jnp.concatenate([lru_ba[l], lru_bx[l]]).reshape(1, 2 * GROUP_WIDTH)
        y_a = _rglru(pa, conv_w[l], conv_b[l], wg, bg, lru_lambda[l])

        qt_n, ksa, vst, kw, vwt, kcvc = _nsa_proj(pb, cos_n, sin_n)
        t16 = kcvc.reshape(B, ncmp, CMP_STRIDE * 2 * NSA_HEAD_DIM)
        kc, vct = _nsa_compress(t16, *_cmp_weights(cmp_pos[l], cmp_w1[l], cmp_b1[l], cmp_w2[l]))
        y_b = _nsa_attn(qt_n, pb, kc, vct, ovl, ksa, vst, kw, vwt)

        wg2_p = _pad_to(gla_wg2[l], (LANES, GLA_HEADS * GLA_DK))
        y_c = _gla(pc, wg2_p, gla_bg2[l], jnp.tile(gla_norm[l], GLA_HEADS))

        wq_p, wkv_p = _mla_weights(mla_w_uq[l], mla_w_ukv[l])
        qt_m, k_m, vt_m = _mla_proj(pd, cos_m, sin_m, _pad_to(mla_q_norm[l], (256,)).reshape(1, 256), mla_kv_norm[l],
                                    wq_p, wkv_p)
        y_d = _mla_attn(qt_m, k_m, vt_m)

        ys = [y.reshape(T, GROUP_WIDTH) for y in (y_a, y_b, y_c, y_d)]
        x2 = _outproj(ys, x2, group_norm[l], w_out[l].astype(BF16))
        x2 = _ffn(x2, norm_ffn[l], w_gate_up[l].astype(BF16), w_down[l].astype(BF16), final_norm,
                  final_norm=(l == depth - 1))
    return x2.reshape(B, S, D)
```

```python
import functools

import numpy as np
import jax
import jax.numpy as jnp
from jax import lax
from jax.experimental import pallas as pl
from jax.experimental.pallas import tpu as pltpu

F32 = jnp.float32
BF16 = jnp.bfloat16

D_MODEL = 1024
GROUP_WIDTH = 256
ROPE_THETA = 10000.0
NORM_EPS = 1e-6
NEG_INF = -1e30
FORCE_SCORE = 1e9

LRU_C = 8.0
CONV_WIDTH = 4

NSA_HEADS = 4
NSA_HEAD_DIM = 64
CMP_LEN = 32
CMP_STRIDE = 16
CMP_HIDDEN = 256
SEL_LEN = 64
SEL_TOPN = 16
WINDOW = 512
NSA_TQ = 256
NSA_TK = 512
NSA_CHAINS = 2
MAX_SEL_BLOCKS = 128

GLA_HEADS = 4
GLA_DV = 64
GLA_DK = 32
GLA_GATE_RANK = 16
GLA_TAU = 16.0
GLA_TILE = 256
GLA_SUB = 16

MLA_HEADS = 4
MLA_V_DIM = 64
MLA_NOPE_DIM = 64
MLA_ROPE_DIM = 32
MLA_QK_DIM = 96
MLA_Q_RANK = 192
MLA_KV_RANK = 128
MLA_TQ = 512
MLA_TK = 512

D_FF = 2816
FF_CHUNK = 256

LANES = 128
V_ONES = 64
LOG2E = 1.4426950408889634
VMEM_LIMIT = 56 * 1024 * 1024

_OFF = {}
_o = 0
for _n, _w in (("a_x", 256), ("a_gate", 256), ("b_q", 256), ("k_c", 64), ("v_c", 64), ("k_s", 64), ("v_s", 64),
               ("k_w", 64), ("v_w", 64), ("b_gate", 12), ("c_q", 128), ("c_k", 128), ("c_v", 256), ("c_glr", 16),
               ("c_og", 256), ("d_cq", 192), ("d_ckv", 128), ("d_kr", 32)):
    _OFF[_n] = (_o, _w)
    _o += _w
IN_COLS = _o
_LAYOUT_A = (("a_x", 256), ("a_gate", 256))
_LAYOUT_B = (("b_q", 256), ("k_s", 64), ("k_c", 64), ("k_w", 64), (None, 64),
             ("v_c", 64), ("v_s", 64), ("v_w", 64), ("b_gate", 12), (None, 52))
_LAYOUT_C = (("c_q", 128), ("c_k", 128), ("c_v", 256), ("c_og", 256), ("c_glr", 16), (None, 112))
_LAYOUT_D = (("d_cq", 192), (None, 64), ("d_ckv", 128), (None, 64), ("d_kr", 32), (None, 32))
WA, WB, WC, WD = 512, 768, 896, 512


def _cparams(sem):
    return pltpu.CompilerParams(dimension_semantics=sem, vmem_limit_bytes=VMEM_LIMIT)


def _dot(a, b):
    return jnp.dot(a, b, preferred_element_type=F32)


def _dot_nt(a, b):
    return lax.dot_general(a, b, (((1,), (1,)), ((), ())), preferred_element_type=F32)


def _dot_tn(a, b):
    return lax.dot_general(a, b, (((0,), (0,)), ((), ())), preferred_element_type=F32)


def _split2(x):
    hi = x.astype(BF16)
    return hi, (x - hi.astype(F32)).astype(BF16)


def _dot_split(a, b, lhs_split=False):
    if lhs_split:
        hi, lo = _split2(a)
        return _dot(hi, b) + _dot(lo, b)
    hi, lo = _split2(b)
    return _dot(a, hi) + _dot(a, lo)


def _dot_x3(a, b):
    a_hi, a_lo = _split2(a)
    b_hi, b_lo = _split2(b)
    return _dot(a_hi, b_hi) + (_dot(a_hi, b_lo) + _dot(a_lo, b_hi))


def _rms(x, width):
    return x * lax.rsqrt(jnp.sum(x * x, axis=-1, keepdims=True) / width + NORM_EPS)


def _inproj_kernel(x_ref, g_ref, w_ref, oa_ref, ob_ref, oc_ref, od_ref):
    h = (_rms(x_ref[...], D_MODEL) * g_ref[...]).astype(BF16)
    off = 0
    for o_ref in (oa_ref, ob_ref, oc_ref, od_ref):
        w = o_ref.shape[1]
        o_ref[...] = _dot(h, w_ref[:, off:off + w])
        off += w


def _inproj(x2, gain, w_p, tm=512):
    T = x2.shape[0]
    ntot = w_p.shape[1]
    return pl.pallas_call(
        _inproj_kernel,
        grid=(T // tm,),
        in_specs=[pl.BlockSpec((tm, D_MODEL), lambda i: (i, 0)),
                  pl.BlockSpec((1, D_MODEL), lambda i: (0, 0)),
                  pl.BlockSpec((D_MODEL, ntot), lambda i: (0, 0))],
        out_specs=[pl.BlockSpec((tm, w), lambda i: (i, 0)) for w in (WA, WB, WC, WD)],
        out_shape=[jax.ShapeDtypeStruct((T, w), F32) for w in (WA, WB, WC, WD)],
        compiler_params=_cparams(("parallel",)),
        name="inproj",
    )(x2, gain.reshape(1, D_MODEL), w_p)


def _rglru_kernel(pa_ref, cw_ref, cb_ref, wg_ref, bg_ref, lam_ref, o_ref, xbuf, a_s, u_s, h_s, hlast):
    ts = pa_ref.shape[1]
    W = GROUP_WIDTH

    @pl.when(pl.program_id(1) == 0)
    def _():
        xbuf[0:8, :] = jnp.zeros((8, W), F32)
        hlast[...] = jnp.zeros_like(hlast)

    xbuf[8:8 + ts, :] = pa_ref[0, :, 0:W]
    xc = cb_ref[...]
    for k in range(CONV_WIDTH):
        lo = 8 - (CONV_WIDTH - 1) + k
        xc = xc + cw_ref[k:k + 1, :] * xbuf[lo:lo + ts, :]
    xbuf[0:8, :] = xbuf[ts:ts + 8, :]
    gates = _dot(xc.astype(BF16), wg_ref[...]) + bg_ref[...]
    r = jax.nn.sigmoid(gates[:, 0:W])
    i = jax.nn.sigmoid(gates[:, W:2 * W])
    log_a = (-LRU_C) * r * jax.nn.softplus(-lam_ref[...])
    a = jnp.exp(log_a)
    u = jnp.sqrt(-jnp.tanh(log_a) * (a * a + 1.0)) * (i * xc)
    a_s[...] = a
    u_s[...] = u

    def body(t, h):
        h = a_s[pl.ds(t, 1), :] * h + u_s[pl.ds(t, 1), :]
        h_s[pl.ds(t, 1), :] = h
        return h

    hlast[...] = lax.fori_loop(0, ts, body, hlast[...], unroll=8)
    o_ref[0] = h_s[...] * jax.nn.gelu(pa_ref[0, :, W:2 * W])


def _rglru(pa, conv_w, conv_b, wg, bg, lam, ts=512):
    B, S, _ = pa.shape
    W = GROUP_WIDTH
    full = lambda shape: pl.BlockSpec(shape, lambda b, j: (0,) * len(shape))
    return pl.pallas_call(
        _rglru_kernel,
        grid=(B, S // ts),
        in_specs=[pl.BlockSpec((1, ts, WA), lambda b, j: (b, j, 0)),
                  full((CONV_WIDTH, W)), full((1, W)), full((W, 2 * W)), full((1, 2 * W)), full((1, W))],
        out_specs=pl.BlockSpec((1, ts, W), lambda b, j: (b, j, 0)),
        out_shape=jax.ShapeDtypeStruct((B, S, W), F32),
        scratch_shapes=[pltpu.VMEM((ts + 8, W), F32), pltpu.VMEM((ts, W), F32), pltpu.VMEM((ts, W), F32),
                        pltpu.VMEM((ts, W), F32), pltpu.VMEM((1, W), F32)],
        compiler_params=_cparams(("parallel", "arbitrary")),
        name="rglru",
    )(pa, conv_w, conv_b.reshape(1, W), wg, bg, lam.reshape(1, W))


def _gla_kernel(pc_ref, wg2_ref, bg2_ref, gn_ref, lcs_ref, hsum_ref, gmean_ref, smask_ref, o_ref, st_ref):
    TT, SUB = GLA_TILE, GLA_SUB
    NB = TT // SUB
    KW = GLA_HEADS * GLA_DK
    VW = GLA_HEADS * GLA_DV

    @pl.when(pl.program_id(1) == 0)
    def _():
        st_ref[...] = jnp.zeros_like(st_ref)

    q = pc_ref[0, :, 0:KW] * (GLA_DK ** -0.5)
    k = pc_ref[0, :, KW:2 * KW]
    v = pc_ref[0, :, 2 * KW:2 * KW + VW]
    og = pc_ref[0, :, 2 * KW + VW:2 * KW + 2 * VW]
    glr = pc_ref[0, :, 2 * KW + 2 * VW:2 * KW + 2 * VW + LANES]
    log_a = jax.nn.log_sigmoid(_dot_x3(glr, wg2_ref[...]) + bg2_ref[...]) * (1.0 / GLA_TAU)
    cums = _dot_split(lcs_ref[...], log_a)
    b = cums[0:TT]
    bl = cums[TT:2 * TT]

    q3 = q.reshape(NB, SUB, KW)
    k3 = k.reshape(NB, SUB, KW)
    b3 = b.reshape(NB, SUB, KW)
    v3 = v.reshape(NB, SUB, VW)
    row = lax.broadcasted_iota(jnp.int32, (NB, SUB, KW), 1)
    terms = []
    for j in range(SUB):
        kj = jnp.broadcast_to(k3[:, j:j + 1, :], (NB, SUB, KW))
        bj = jnp.broadcast_to(b3[:, j:j + 1, :], (NB, SUB, KW))
        e = q3 * kj * jnp.exp(jnp.where(row >= j, b3 - bj, NEG_INF))
        terms.append(e.reshape(TT, KW))
    e_all = jnp.concatenate(terms, axis=0)
    attn = _dot(e_all.astype(BF16), hsum_ref[...])
    o = jnp.zeros((NB, SUB, VW), F32)
    for j in range(SUB):
        vj = jnp.broadcast_to(v3[:, j:j + 1, :], (NB, SUB, VW))
        o = o + attn[j * TT:(j + 1) * TT, :].reshape(NB, SUB, VW) * vj
    o = o.reshape(TT, VW)

    qd = (q * jnp.exp(b)).astype(BF16)
    kd = (k * jnp.exp(bl - b)).astype(BF16)
    dec = jnp.exp(bl)
    vb = v.astype(BF16)
    blocks = [slice(n * SUB, (n + 1) * SUB) for n in range(NB)]
    upds = [_dot_tn(vb[rows], kd[rows]) * smask_ref[...] for rows in blocks]
    st = st_ref[...]
    sts = []
    for n in range(NB):
        sts.append(st.astype(BF16))
        st = st * dec[n * SUB:n * SUB + 1, :] + upds[n]
    st_ref[...] = st
    o = o + jnp.concatenate([_dot_nt(qd[rows], s_n) for rows, s_n in zip(blocks, sts)], axis=0)

    ms = _dot_split(o * o, gmean_ref[...], lhs_split=True)
    o_ref[0] = o * lax.rsqrt(ms + NORM_EPS) * gn_ref[...] * jax.nn.silu(og)


def _gla_consts():
    TT, SUB = GLA_TILE, GLA_SUB
    KW, VW = GLA_HEADS * GLA_DK, GLA_HEADS * GLA_DV
    i = np.arange(TT)
    same = (i[:, None] // SUB) == (i[None, :] // SUB)
    lcum = (same & (i[None, :] <= i[:, None])).astype(np.float32)
    lsum = same.astype(np.float32)
    hk = np.arange(KW) // GLA_DK
    hv = np.arange(VW) // GLA_DV
    hsum = (hk[:, None] == hv[None, :]).astype(np.float32)
    gmean = (hv[:, None] == hv[None, :]).astype(np.float32) / GLA_DV
    smask = (hv[:, None] == hk[None, :]).astype(np.float32)
    return (jnp.asarray(np.concatenate([lcum, lsum], axis=0), dtype=BF16), jnp.asarray(hsum, dtype=BF16),
            jnp.asarray(gmean, dtype=BF16), jnp.asarray(smask))


def _gla(pc, wg2_p, bg2, gn_t):
    B, S, _ = pc.shape
    TT = GLA_TILE
    KW, VW = GLA_HEADS * GLA_DK, GLA_HEADS * GLA_DV
    lcs, hsum, gmean, smask = _gla_consts()
    full = lambda shape: pl.BlockSpec(shape, lambda b, j: (0,) * len(shape))
    return pl.pallas_call(
        _gla_kernel,
        grid=(B, S // TT),
        in_specs=[pl.BlockSpec((1, TT, WC), lambda b, j: (b, j, 0)),
                  full((LANES, KW)), full((1, KW)), full((1, VW)), full((2 * TT, TT)),
                  full((KW, VW)), full((VW, VW)), full((VW, KW))],
        out_specs=pl.BlockSpec((1, TT, VW), lambda b, j: (b, j, 0)),
        out_shape=jax.ShapeDtypeStruct((B, S, VW), F32),
        scratch_shapes=[pltpu.VMEM((VW, KW), F32)],
        compiler_params=_cparams(("parallel", "arbitrary")),
        name="gla",
    )(pc, wg2_p, bg2.reshape(1, KW), gn_t.reshape(1, VW), lcs, hsum, gmean, smask)


def _rope_lanes(x, cos, sin_signed, half):
    lane = lax.broadcasted_iota(jnp.int32, cos.shape, 1)
    lo = (lane % (2 * half)) < half
    outs = []
    for c in range(x.shape[1] // LANES):
        xs = x[:, c * LANES:(c + 1) * LANES]
        rot = jnp.where(lo, pltpu.roll(xs, LANES - half, 1), pltpu.roll(xs, half, 1))
        outs.append(xs * cos + rot * sin_signed)
    return outs[0] if len(outs) == 1 else jnp.concatenate(outs, axis=1)


def _rope_table_kernel(pos_ref, invf_ref, cos_ref, sin_ref, *, half):
    ang = pos_ref[0].astype(F32) * invf_ref[...]
    lane = lax.broadcasted_iota(jnp.int32, ang.shape, 1)
    sin = jnp.sin(ang)
    cos_ref[0] = jnp.cos(ang)
    sin_ref[0] = jnp.where((lane % (2 * half)) < half, -sin, sin)


def _rope_tables(pos3, d, lead_zeros=0, tail_zeros=0, tm=1024):
    B, S, _ = pos3.shape
    inv = ROPE_THETA ** (-jnp.arange(0, d, 2, dtype=F32) / d)
    reps = (LANES - lead_zeros - tail_zeros) // d
    invf = jnp.concatenate([jnp.zeros((lead_zeros,), F32), jnp.tile(jnp.concatenate([inv, inv]), reps),
                            jnp.zeros((tail_zeros,), F32)]).reshape(1, LANES)
    spec = pl.BlockSpec((1, tm, LANES), lambda b, j: (b, j, 0))
    shape = jax.ShapeDtypeStruct((B, S, LANES), F32)
    return pl.pallas_call(
        functools.partial(_rope_table_kernel, half=d // 2),
        grid=(B, S // tm),
        in_specs=[pl.BlockSpec((1, tm, 1), lambda b, j: (b, j, 0)), pl.BlockSpec((1, LANES), lambda b, j: (0, 0))],
        out_specs=[spec, spec],
        out_shape=[shape, shape],
        compiler_params=_cparams(("parallel", "parallel")),
        name="rope_tables",
    )(pos3, invf)


def _mla_proj_kernel(pd_ref, cos_ref, sin_ref, qn_ref, kvn_ref, wq_ref, wkv_ref, qt_ref, k_ref, vt_ref):
    H = MLA_HEADS
    tm = pd_ref.shape[1]
    cos, sin = cos_ref[0], sin_ref[0]
    cq = (_rms(pd_ref[0, :, 0:256], MLA_Q_RANK) * qn_ref[...]).astype(BF16)
    ckv = (_rms(pd_ref[0, :, 256:384], MLA_KV_RANK) * kvn_ref[...]).astype(BF16)
    kr = _rope_lanes(pd_ref[0, :, 384:512], cos, sin, MLA_ROPE_DIM // 2)
    q_all = _dot(cq, wq_ref[...])
    kv_all = _dot(ckv, wkv_ref[...])
    low = lax.broadcasted_iota(jnp.int32, (tm, LANES), 1) < MLA_V_DIM
    for h in range(H):
        qh = _rope_lanes(q_all[:, h * LANES:(h + 1) * LANES], cos, sin, MLA_ROPE_DIM // 2)
        qt_ref[0, h, 0] = (qh * (MLA_QK_DIM ** -0.5 * LOG2E)).T.astype(BF16)
        k_ref[0, h] = (kv_all[:, h * LANES:(h + 1) * LANES] + kr).astype(BF16)
        vt_ref[0, h, 0] = jnp.where(low, kv_all[:, (H + h) * LANES:(H + h + 1) * LANES], 1.0).T.astype(BF16)


def _mla_proj(pd, cos, sin, qn_p, kvn, wq_p, wkv_p):
    B, S, _ = pd.shape
    H = MLA_HEADS
    tm = MLA_TK
    assert MLA_TQ == tm
    full = lambda shape: pl.BlockSpec(shape, lambda b, j: (0,) * len(shape))
    tspec = pl.BlockSpec((1, H, 1, LANES, tm), lambda b, j: (b, 0, j, 0, 0))
    tshape = jax.ShapeDtypeStruct((B, H, S // tm, LANES, tm), BF16)
    return pl.pallas_call(
        _mla_proj_kernel,
        grid=(B, S // tm),
        in_specs=[pl.BlockSpec((1, tm, WD), lambda b, j: (b, j, 0)),
                  pl.BlockSpec((1, tm, LANES), lambda b, j: (b, j, 0)),
                  pl.BlockSpec((1, tm, LANES), lambda b, j: (b, j, 0)),
                  full((1, 256)), full((1, LANES)), full((256, H * LANES)), full((LANES, 2 * H * LANES))],
        out_specs=[tspec, pl.BlockSpec((1, H, tm, LANES), lambda b, j: (b, 0, j, 0)), tspec],
        out_shape=[tshape, jax.ShapeDtypeStruct((B, H, S, LANES), BF16), tshape],
        compiler_params=_cparams(("parallel", "parallel")),
        name="mla_proj",
    )(pd, cos, sin, qn_p, kvn.reshape(1, LANES), wq_p, wkv_p)


def _softmax_update(s_refs, vts, m_refs, acc_refs, mask):
    ss = [r[...] for r in s_refs]
    if mask is not None:
        ss = [jnp.where(mask, s, NEG_INF) for s in ss]
    m_prevs = [r[...] for r in m_refs]
    m_news = [jnp.maximum(mp, jnp.max(s, axis=0, keepdims=True)) for mp, s in zip(m_prevs, ss)]
    ps = [jnp.exp2(s - mn).astype(BF16) for s, mn in zip(ss, m_news)]
    for vt1, m_ref, acc_ref, p, mp, mn in zip(vts, m_refs, acc_refs, ps, m_prevs, m_news):
        acc_ref[...] = jnp.exp2(mp - mn) * acc_ref[...] + _dot(vt1, p)
        m_ref[...] = mn


def _flash_loop(n_full, scores, values, mask, buf_a, buf_b, m_refs, acc_refs):
    for r in m_refs:
        r[...] = jnp.full(r.shape, NEG_INF, F32)
    for r in acc_refs:
        r[...] = jnp.zeros(r.shape, F32)

    def fill(bufs, j):
        for r, sc in zip(bufs, scores(j)):
            r[...] = sc

    fill(buf_a, 0)

    def body(jj, carry):
        j = 2 * jj
        fill(buf_b, j + 1)
        _softmax_update(buf_a, values(j), m_refs, acc_refs, None)
        fill(buf_a, j + 2)
        _softmax_update(buf_b, values(j + 1), m_refs, acc_refs, None)
        return carry

    lax.fori_loop(0, n_full // 2, body, 0)

    @pl.when(n_full % 2 == 1)
    def _():
        fill(buf_b, n_full)
        _softmax_update(buf_a, values(n_full - 1), m_refs, acc_refs, None)
        _softmax_update(buf_b, values(n_full), m_refs, acc_refs, mask)

    @pl.when(n_full % 2 == 0)
    def _():
        _softmax_update(buf_a, values(n_full), m_refs, acc_refs, mask)


def _normalize(acc):
    return acc / jnp.maximum(acc[V_ONES:V_ONES + 1, :], 1e-30)


def _mla_attn_kernel(qt_ref, k_ref, vt_ref, o_ref, *scratch):
    tq, tk = MLA_TQ, MLA_TK
    H = MLA_HEADS
    m_refs, acc_refs, buf_a, buf_b = (scratch[i * H:(i + 1) * H] for i in range(4))
    s0 = pl.program_id(1) * tq
    n_full = s0 // tk

    def scores(j):
        off = pl.multiple_of(j * tk, tk)
        return [_dot(k_ref[0, h, pl.ds(off, tk), :], qt_ref[0, h, 0]) for h in range(H)]

    def values(j):
        return [vt_ref[0, h, j] for h in range(H)]

    kpos = n_full * tk + lax.broadcasted_iota(jnp.int32, (tk, tq), 0)
    t = s0 + lax.broadcasted_iota(jnp.int32, (tk, tq), 1)
    _flash_loop(n_full, scores, values, kpos <= t, buf_a, buf_b, m_refs, acc_refs)
    o_t = jnp.concatenate([_normalize(acc_refs[h][...])[0:MLA_V_DIM, :] for h in range(H)], axis=0)
    o_ref[0] = o_t.T


def _mla_attn(qt, k, vt):
    B, H, S, _ = k.shape
    tq, tk = MLA_TQ, MLA_TK
    return pl.pallas_call(
        _mla_attn_kernel,
        grid=(B, S // tq),
        in_specs=[pl.BlockSpec((1, H, 1, LANES, tq), lambda b, i: (b, 0, i, 0, 0)),
                  pl.BlockSpec((1, H, S, LANES), lambda b, i: (b, 0, 0, 0)),
                  pl.BlockSpec((1, H, S // tk, LANES, tk), lambda b, i: (b, 0, 0, 0, 0))],
        out_specs=pl.BlockSpec((1, tq, H * MLA_V_DIM), lambda b, i: (b, i, 0)),
        out_shape=jax.ShapeDtypeStruct((B, S, H * MLA_V_DIM), F32),
        scratch_shapes=([pltpu.VMEM((1, tq), F32)] * H + [pltpu.VMEM((LANES, tq), F32)] * H
                        + [pltpu.VMEM((tk, tq), F32)] * (2 * H)),
        compiler_params=_cparams(("parallel", "arbitrary")),
        name="mla_attn",
    )(qt, k, vt)


def _nsa_proj_kernel(pb_ref, cos_ref, sin_ref, qt_ref, ksa_ref, vst_ref, kw_ref, vwt_ref, kcvc_ref):
    tm = pb_ref.shape[1]
    H, TQ = NSA_HEADS, NSA_TQ
    r = _rope_lanes(pb_ref[0, :, 0:512], cos_ref[0], sin_ref[0], NSA_HEAD_DIM // 2)
    nr0 = pb_ref[0, :, 512:640]
    nr1 = pb_ref[0, :, 640:768]
    lane = lax.broadcasted_iota(jnp.int32, (tm, LANES), 1)
    low = lane < NSA_HEAD_DIM
    scale = NSA_HEAD_DIM ** -0.5 * LOG2E
    for h in range(H):
        seg = r[:, (h // 2) * LANES:(h // 2 + 1) * LANES]
        if h % 2:
            seg = pltpu.roll(seg, NSA_HEAD_DIM, 1)
        qh_t = jnp.where(low, seg * scale, 0.0).T.astype(BF16)
        for c in range(tm // TQ):
            qt_ref[0, c, :, h * TQ:(h + 1) * TQ] = qh_t[:, c * TQ:(c + 1) * TQ]
    kseg = r[:, 256:384]
    blk = lax.shift_right_logical(lax.broadcasted_iota(jnp.int32, (tm, LANES), 0) + pl.program_id(1) * tm, 6)
    ksa_ref[0, :, 0:LANES] = jnp.where(low, kseg, 0.0).astype(BF16)
    ksa_ref[0, :, LANES:2 * LANES] = jnp.where(blk == lane, 1.0, 0.0).astype(BF16)
    kw_ref[0] = r[:, 384:512].astype(BF16)
    vst_ref[0, 0] = jnp.where(low, pltpu.roll(nr0, NSA_HEAD_DIM, 1), 1.0).T.astype(BF16)
    vw_t = jnp.where(low, nr1, 1.0).T.astype(BF16)
    for c in range(tm // LANES):
        vwt_ref[0, c] = vw_t[:, c * LANES:(c + 1) * LANES]
    kcvc_ref[0] = jnp.where(low, pltpu.roll(kseg, NSA_HEAD_DIM, 1), pltpu.roll(nr0, NSA_HEAD_DIM, 1))


def _nsa_proj(pb, cos, sin):
    B, S, _ = pb.shape
    H, TQ = NSA_HEADS, NSA_TQ
    tm = NSA_TK
    tok = lambda w: pl.BlockSpec((1, tm, w), lambda b, j: (b, j, 0))
    return pl.pallas_call(
        _nsa_proj_kernel,
        grid=(B, S // tm),
        in_specs=[tok(WB), tok(LANES), tok(LANES)],
        out_specs=[pl.BlockSpec((1, tm // TQ, LANES, H * TQ), lambda b, j: (b, j, 0, 0)),
                   tok(2 * LANES),
                   pl.BlockSpec((1, 1, LANES, tm), lambda b, j: (b, j, 0, 0)),
                   tok(LANES),
                   pl.BlockSpec((1, tm // LANES, LANES, LANES), lambda b, j: (b, j, 0, 0)),
                   tok(LANES)],
        out_shape=[jax.ShapeDtypeStruct((B, S // TQ, LANES, H * TQ), BF16),
                   jax.ShapeDtypeStruct((B, S, 2 * LANES), BF16),
                   jax.ShapeDtypeStruct((B, S // tm, LANES, tm), BF16),
                   jax.ShapeDtypeStruct((B, S, LANES), BF16),
                   jax.ShapeDtypeStruct((B, S // LANES, LANES, LANES), BF16),
                   jax.ShapeDtypeStruct((B, S, LANES), F32)],
        compiler_params=_cparams(("parallel", "parallel")),
        name="nsa_proj",
    )(pb, cos, sin)


def _nsa_cmp_kernel(t_ref, ptop_ref, pbot_ref, wtop_ref, wbot_ref, b1_ref, w2k_ref, w2v_ref, kc_ref, vct_ref, sh_ref):
    n = t_ref.shape[1]
    t = t_ref[0]
    top = _dot((t + ptop_ref[...]).astype(BF16), wtop_ref[...])
    sh_ref[0:n, :] = _dot((t + pbot_ref[...]).astype(BF16), wbot_ref[...])
    sh_ref[n:n + 8, :] = jnp.zeros((8, sh_ref.shape[1]), F32)
    hid = jax.nn.gelu(top + sh_ref[pl.ds(1, n), :] + b1_ref[...])
    kc_ref[0] = _dot(hid[:, 0:CMP_HIDDEN].astype(BF16), w2k_ref[...]).astype(BF16)
    vc = _dot(hid[:, CMP_HIDDEN:2 * CMP_HIDDEN].astype(BF16), w2v_ref[...])
    low = lax.broadcasted_iota(jnp.int32, vc.shape, 1) < NSA_HEAD_DIM
    vct_ref[0] = jnp.where(low, vc, 1.0).T.astype(BF16)


def _nsa_compress(t16, ptop, pbot, wtop, wbot, b1, w2k, w2v):
    B, n, F = t16.shape
    full = lambda shape: pl.BlockSpec(shape, lambda b: (0,) * len(shape))
    return pl.pallas_call(
        _nsa_cmp_kernel,
        grid=(B,),
        in_specs=[pl.BlockSpec((1, n, F), lambda b: (b, 0, 0)), full((1, F)), full((1, F)),
                  full((F, 2 * CMP_HIDDEN)), full((F, 2 * CMP_HIDDEN)), full((1, 2 * CMP_HIDDEN)),
                  full((CMP_HIDDEN, LANES)), full((CMP_HIDDEN, LANES))],
        out_specs=[pl.BlockSpec((1, n, LANES), lambda b: (b, 0, 0)), pl.BlockSpec((1, LANES, n), lambda b: (b, 0, 0))],
        out_shape=[jax.ShapeDtypeStruct((B, n, LANES), BF16), jax.ShapeDtypeStruct((B, LANES, n), BF16)],
        scratch_shapes=[pltpu.VMEM((n + 8, 2 * CMP_HIDDEN), F32)],
        compiler_params=_cparams(("parallel",)),
        name="nsa_compress",
    )(t16, ptop, pbot, wtop, wbot, b1, w2k, w2v)


def _nsa_attn_kernel(qt_ref, g_ref, kc_ref, vct_ref, ovl_ref, ksa_ref, vst_ref, kw_ref, vwt_ref, o_ref, *scratch):
    H, TQ, TK = NSA_HEADS, NSA_TQ, NSA_TK
    M = H * TQ
    NB = MAX_SEL_BLOCKS
    s0 = pl.program_id(1) * TQ
    ncmp = kc_ref.shape[1]
    qt = qt_ref[0, 0]
    t_row = s0 + lax.broadcasted_iota(jnp.int32, (1, M), 1) % TQ

    cend = lax.broadcasted_iota(jnp.int32, (ncmp, M), 0) * CMP_STRIDE + (CMP_LEN - 1)
    s_c = jnp.where(cend <= t_row, _dot(kc_ref[0], qt), NEG_INF)
    m_c = jnp.max(s_c, axis=0, keepdims=True)
    e_c = jnp.exp2(s_c - m_c)
    a_c = _dot(vct_ref[0], e_c.astype(BF16))
    inv_c = jnp.where(m_c > 0.5 * NEG_INF, 1.0 / jnp.maximum(a_c[V_ONES:V_ONES + 1, :], 1e-30), 0.0)
    o_c = a_c * inv_c
    p_c = e_c * inv_c

    WK = WINDOW + TQ
    w0 = pl.multiple_of(jnp.maximum(s0 - WINDOW, 0), TQ)
    dist = t_row - (w0 + lax.broadcasted_iota(jnp.int32, (WK, M), 0))
    in_window = lax.bitcast_convert_type(dist, jnp.uint32) < jnp.uint32(WINDOW)
    s_w = jnp.where(in_window, _dot(kw_ref[0, pl.ds(w0, WK), :], qt), NEG_INF)
    e_w = jnp.exp2(s_w - jnp.max(s_w, axis=0, keepdims=True)).astype(BF16)
    a_w = _dot(vwt_ref[0, w0 // LANES], e_w[0:LANES])
    for c in range(1, WK // LANES):
        a_w = a_w + _dot(vwt_ref[0, w0 // LANES + c], e_w[c * LANES:(c + 1) * LANES])
    o_w = _normalize(a_w)

    psum = p_c[:, 0:TQ] + p_c[:, TQ:2 * TQ] + p_c[:, 2 * TQ:3 * TQ] + p_c[:, 3 * TQ:4 * TQ]
    p_hi = psum.astype(BF16)
    p_lo = (psum - p_hi.astype(F32)).astype(BF16)
    imp = _dot(ovl_ref[...], p_hi) + _dot(ovl_ref[...], p_lo)
    jb = lax.broadcasted_iota(jnp.int32, (NB, TQ), 0)
    tq_l = s0 + lax.broadcasted_iota(jnp.int32, (NB, TQ), 1)
    cur = lax.shift_right_logical(tq_l, 6)
    forced = (jb == 0) | (jb == cur) | (jb == cur - 1)
    score = jnp.where(forced, -3e38, jnp.where(jb * SEL_LEN <= tq_l, imp, NEG_INF))
    jbf = jb.astype(F32)
    sel = jnp.where(forced, 1.0, 0.0)
    for _ in range(SEL_TOPN - 3):
        best = jnp.max(score, axis=0, keepdims=True)
        first = jnp.min(jnp.where(score == best, jbf, float(NB)), axis=0, keepdims=True)
        hit = jbf == first
        sel = jnp.where(hit, 1.0, sel)
        score = jnp.where(hit, -3e38, score)
    bias = ((sel - 1.0) * (-NEG_INF)).astype(BF16)
    q_aug = jnp.concatenate([qt, jnp.concatenate([bias] * H, axis=1)], axis=0)

    NC = NSA_CHAINS
    CW = M // NC
    m_refs, acc_refs, buf_a, buf_b = (scratch[i * NC:(i + 1) * NC] for i in range(4))
    n_full = s0 // TK
    q_cols = [q_aug[:, c * CW:(c + 1) * CW] for c in range(NC)]

    def scores(j):
        k = ksa_ref[0, pl.ds(pl.multiple_of(j * TK, TK), TK), :]
        return [_dot(k, qc) for qc in q_cols]

    def values(j):
        return [vst_ref[0, j]] * NC

    kpos = n_full * TK + lax.broadcasted_iota(jnp.int32, (TK, CW), 0)
    _flash_loop(n_full, scores, values, kpos <= t_row[:, 0:CW], buf_a, buf_b, m_refs, acc_refs)
    o_s = jnp.concatenate([_normalize(acc_refs[c][...]) for c in range(NC)], axis=1)

    g = jax.nn.sigmoid(g_ref[0]).T
    outs = []
    for h in range(H):
        cols = slice(h * TQ, (h + 1) * TQ)
        r0 = NSA_HEAD_DIM + 3 * h
        o_h = (g[r0:r0 + 1, :] * o_c[0:NSA_HEAD_DIM, cols] + g[r0 + 1:r0 + 2, :] * o_s[0:NSA_HEAD_DIM, cols]
               + g[r0 + 2:r0 + 3, :] * o_w[0:NSA_HEAD_DIM, cols])
        outs.append(o_h)
    o_ref[0] = jnp.concatenate(outs, axis=0).T


def _nsa_attn(qt, pb, kc, vct, ovl, ksa, vst, kw, vwt):
    B, S, _ = ksa.shape
    H, TQ, TK = NSA_HEADS, NSA_TQ, NSA_TK
    ncmp = kc.shape[1]
    per_b = lambda *shape: pl.BlockSpec((1,) + shape, lambda b, i: (b,) + (0,) * len(shape))
    return pl.pallas_call(
        _nsa_attn_kernel,
        grid=(B, S // TQ),
        in_specs=[pl.BlockSpec((1, 1, LANES, H * TQ), lambda b, i: (b, i, 0, 0)),
                  pl.BlockSpec((1, TQ, LANES), lambda b, i: (b, i, WB // LANES - 1)),
                  per_b(ncmp, LANES), per_b(LANES, ncmp),
                  pl.BlockSpec((MAX_SEL_BLOCKS, ncmp), lambda b, i: (0, 0)),
                  per_b(S, 2 * LANES), per_b(S // TK, LANES, TK), per_b(S, LANES), per_b(S // LANES, LANES, LANES)],
        out_specs=pl.BlockSpec((1, TQ, H * NSA_HEAD_DIM), lambda b, i: (b, i, 0)),
        out_shape=jax.ShapeDtypeStruct((B, S, H * NSA_HEAD_DIM), F32),
        scratch_shapes=([pltpu.VMEM((1, H * TQ // NSA_CHAINS), F32)] * NSA_CHAINS
                        + [pltpu.VMEM((LANES, H * TQ // NSA_CHAINS), F32)] * NSA_CHAINS
                        + [pltpu.VMEM((TK, H * TQ // NSA_CHAINS), F32)] * (2 * NSA_CHAINS)),
        compiler_params=_cparams(("parallel", "arbitrary")),
        name="nsa_attn",
    )(qt, pb, kc, vct, ovl, ksa, vst, kw, vwt)


def _nsa_overlap(ncmp):
    n = np.arange(ncmp)[None, :] * CMP_STRIDE
    s = np.arange(MAX_SEL_BLOCKS)[:, None] * SEL_LEN
    real = np.arange(ncmp)[None, :] < ncmp - 1
    ovl = (n < s + SEL_LEN) & (n + CMP_LEN > s) & real
    return jnp.asarray(ovl.astype(np.float32), dtype=BF16)


def _outproj_kernel(ya_ref, yb_ref, yc_ref, yd_ref, x_ref, gn_ref, w_ref, o_ref):
    acc = x_ref[...]
    for g, y_ref in enumerate((ya_ref, yb_ref, yc_ref, yd_ref)):
        sl = slice(g * GROUP_WIDTH, (g + 1) * GROUP_WIDTH)
        y = (_rms(y_ref[...], GROUP_WIDTH) * gn_ref[:, sl]).astype(BF16)
        acc = acc + _dot(y, w_ref[sl, :])
    o_ref[...] = acc


def _outproj(ys, x2, gn, w_out_b, tm=512):
    T = x2.shape[0]
    W = GROUP_WIDTH
    return pl.pallas_call(
        _outproj_kernel,
        grid=(T // tm,),
        in_specs=[pl.BlockSpec((tm, W), lambda i: (i, 0))] * 4
        + [pl.BlockSpec((tm, D_MODEL), lambda i: (i, 0)),
           pl.BlockSpec((1, D_MODEL), lambda i: (0, 0)),
           pl.BlockSpec((D_MODEL, D_MODEL), lambda i: (0, 0))],
        out_specs=pl.BlockSpec((tm, D_MODEL), lambda i: (i, 0)),
        out_shape=jax.ShapeDtypeStruct((T, D_MODEL), F32),
        compiler_params=_cparams(("parallel",)),
        name="outproj",
    )(*ys, x2, gn.reshape(1, D_MODEL), w_out_b)


def _ffn_kernel(x_ref, g_ref, wgu_ref, wd_ref, fn_ref, o_ref, *, final_norm):
    x = x_ref[...]
    h = (_rms(x, D_MODEL) * g_ref[...]).astype(BF16)
    acc = x
    for c in range(D_FF // FF_CHUNK):
        lo = c * FF_CHUNK
        gate = _dot(h, wgu_ref[:, lo:lo + FF_CHUNK])
        up = _dot(h, wgu_ref[:, D_FF + lo:D_FF + lo + FF_CHUNK])
        acc = acc + _dot((jax.nn.silu(gate) * up).astype(BF16), wd_ref[lo:lo + FF_CHUNK, :])
    if final_norm:
        acc = _rms(acc, D_MODEL) * fn_ref[...]
    o_ref[...] = acc


def _ffn(x2, gain, wgu_b, wd_b, fnorm, final_norm, tm=512):
    T = x2.shape[0]
    row = pl.BlockSpec((1, D_MODEL), lambda i: (0, 0))
    return pl.pallas_call(
        functools.partial(_ffn_kernel, final_norm=final_norm),
        grid=(T // tm,),
        in_specs=[pl.BlockSpec((tm, D_MODEL), lambda i: (i, 0)), row,
                  pl.BlockSpec((D_MODEL, 2 * D_FF), lambda i: (0, 0)),
                  pl.BlockSpec((D_FF, D_MODEL), lambda i: (0, 0)), row],
        out_specs=pl.BlockSpec((tm, D_MODEL), lambda i: (i, 0)),
        out_shape=jax.ShapeDtypeStruct((T, D_MODEL), F32),
        compiler_params=_cparams(("parallel",)),
        name="ffn",
    )(x2, gain.reshape(1, D_MODEL), wgu_b, wd_b, fnorm.reshape(1, D_MODEL))


def _gather_cols(w, layout):
    parts = []
    for name, width in layout:
        if name is None:
            parts.append(jnp.zeros((w.shape[0], width), w.dtype))
        else:
            o, n = _OFF[name]
            assert n == width
            parts.append(w[:, o:o + n])
    return jnp.concatenate(parts, axis=1)


def _block_diag(w):
    h, d, _ = w.shape
    out = jnp.zeros((h * d, h * d), w.dtype)
    for i in range(h):
        out = out.at[i * d:(i + 1) * d, i * d:(i + 1) * d].set(w[i])
    return out


def _pad_to(a, shape):
    return jnp.pad(a, [(0, s - d) for s, d in zip(shape, a.shape)])


def _mla_weights(w_uq, w_ukv):
    H = MLA_HEADS
    wq = w_uq.reshape(MLA_Q_RANK, H, MLA_QK_DIM)
    wq = _pad_to(wq, (256, H, LANES)).reshape(256, H * LANES)
    wkv = w_ukv.reshape(MLA_KV_RANK, H, MLA_NOPE_DIM + MLA_V_DIM)
    wk = _pad_to(wkv[:, :, :MLA_NOPE_DIM], (MLA_KV_RANK, H, LANES)).reshape(MLA_KV_RANK, H * LANES)
    wv = _pad_to(wkv[:, :, MLA_NOPE_DIM:], (MLA_KV_RANK, H, LANES)).reshape(MLA_KV_RANK, H * LANES)
    return wq.astype(BF16), jnp.concatenate([wk, wv], axis=1).astype(BF16)


def _cmp_weights(cmp_pos, cmp_w1, cmp_b1, cmp_w2):
    half = CMP_LEN // 2
    Dh = NSA_HEAD_DIM
    w1 = cmp_w1.reshape(2, CMP_LEN, Dh, CMP_HIDDEN)

    def rows(part):
        wk = w1[0, part * half:(part + 1) * half]
        wv = w1[1, part * half:(part + 1) * half]
        z = jnp.zeros_like(wk)
        top = jnp.concatenate([wk, z], axis=-1)
        bot = jnp.concatenate([z, wv], axis=-1)
        return jnp.concatenate([top, bot], axis=1).reshape(half * 2 * Dh, 2 * CMP_HIDDEN)

    def pos_row(part):
        p = jnp.concatenate([cmp_pos[0, part * half:(part + 1) * half], cmp_pos[1, part * half:(part + 1) * half]],
                            axis=-1)
        return p.reshape(1, half * 2 * Dh)

    b1 = jnp.concatenate([cmp_b1[0], cmp_b1[1]]).reshape(1, 2 * CMP_HIDDEN)
    w2k = _pad_to(cmp_w2[0], (CMP_HIDDEN, LANES)).astype(BF16)
    w2v = _pad_to(cmp_w2[1], (CMP_HIDDEN, LANES)).astype(BF16)
    return pos_row(0), pos_row(1), rows(0).astype(BF16), rows(1).astype(BF16), b1, w2k, w2v


def kernel(x, positions, norm_mix, w_in, conv_w, conv_b, lru_wa, lru_ba, lru_wx, lru_bx, lru_lambda, cmp_pos, cmp_w1, cmp_b1, cmp_w2, gla_wg2, gla_bg2, gla_norm, mla_q_norm, mla_kv_norm, mla_w_uq, mla_w_ukv, group_norm, w_out, norm_ffn, w_gate_up, w_down, final_norm):
    B, S, D = x.shape
    assert D == D_MODEL and S % (CMP_STRIDE * 8) == 0 and S // SEL_LEN <= MAX_SEL_BLOCKS and S >= WINDOW + NSA_TQ
    depth = w_in.shape[0]
    T = B * S
    pos3 = positions.reshape(B, S, 1)
    ncmp = S // CMP_STRIDE
    ovl = _nsa_overlap(ncmp)
    cos_n, sin_n = _rope_tables(pos3, NSA_HEAD_DIM)
    cos_m, sin_m = _rope_tables(pos3, MLA_ROPE_DIM, lead_zeros=MLA_NOPE_DIM, tail_zeros=LANES - MLA_QK_DIM)
    x2 = x.reshape(T, D)
    for l in range(depth):
        w_p = jnp.concatenate([_gather_cols(w_in[l], lay) for lay in (_LAYOUT_A, _LAYOUT_B, _LAYOUT_C, _LAYOUT_D)],
                              axis=1).astype(BF16)
        pa, pb, pc, pd = _inproj(x2, norm_mix[l], w_p)
        pa, pb, pc, pd = (p.reshape(B, S, -1) for p in (pa, pb, pc, pd))

        wg = jnp.concatenate([_block_diag(lru_wa[l]), _block_diag(lru_wx[l])], axis=1).astype(BF16)
        bg = jnp.concatenate([lru_ba[l], lru_bx[l]]).reshape(1, 2 * GROUP_WIDTH)
        y_a = _rglru(pa, conv_w[l], conv_b[l], wg, bg, lru_lambda[l])

        qt_n, ksa, vst, kw, vwt, kcvc = _nsa_proj(pb, cos_n, sin_n)
        t16 = kcvc.reshape(B, ncmp, CMP_STRIDE * 2 * NSA_HEAD_DIM)
        kc, vct = _nsa_compress(t16, *_cmp_weights(cmp_pos[l], cmp_w1[l], cmp_b1[l], cmp_w2[l]))
        y_b = _nsa_attn(qt_n, pb, kc, vct, ovl, ksa, vst, kw, vwt)

        wg2_p = _pad_to(gla_wg2[l], (LANES, GLA_HEADS * GLA_DK))
        y_c = _gla(pc, wg2_p, gla_bg2[l], jnp.tile(gla_norm[l], GLA_HEADS))

        wq_p, wkv_p = _mla_weights(mla_w_uq[l], mla_w_ukv[l])
        qt_m, k_m, vt_m = _mla_proj(pd, cos_m, sin_m, _pad_to(mla_q_norm[l], (256,)).reshape(1, 256), mla_kv_norm[l],
                                    wq_p, wkv_p)
        y_d = _mla_attn(qt_m, k_m, vt_m)

        ys = [y.reshape(T, GROUP_WIDTH) for y in (y_a, y_b, y_c, y_d)]
        x2 = _outproj(ys, x2, group_norm[l], w_out[l].astype(BF16))
        x2 = _ffn(x2, norm_ffn[l], w_gate_up[l].astype(BF16), w_down[l].astype(BF16), final_norm,
                  final_norm=(l == depth - 1))
    return x2.reshape(B, S, D)
```

```python
import functools

import numpy as np
import jax
import jax.numpy as jnp
from jax import lax
from jax.experimental import pallas as pl
from jax.experimental.pallas import tpu as pltpu

F32 = jnp.float32
BF16 = jnp.bfloat16

D_MODEL = 1024
GROUP_WIDTH = 256
ROPE_THETA = 10000.0
NORM_EPS = 1e-6
NEG_INF = -1e30
FORCE_SCORE = 1e9

LRU_C = 8.0
CONV_WIDTH = 4

NSA_HEADS = 4
NSA_HEAD_DIM = 64
CMP_LEN = 32
CMP_STRIDE = 16
CMP_HIDDEN = 256
SEL_LEN = 64
SEL_TOPN = 16
WINDOW = 512
NSA_TQ = 256
NSA_TK = 512
NSA_CHAINS = 2
MAX_SEL_BLOCKS = 128

GLA_HEADS = 4
GLA_DV = 64
GLA_DK = 32
GLA_GATE_RANK = 16
GLA_TAU = 16.0
GLA_TILE = 256
GLA_SUB = 16

MLA_HEADS = 4
MLA_V_DIM = 64
MLA_NOPE_DIM = 64
MLA_ROPE_DIM = 32
MLA_QK_DIM = 96
MLA_Q_RANK = 192
MLA_KV_RANK = 128
MLA_TQ = 512
MLA_TK = 512

D_FF = 2816
FF_CHUNK = 256

LANES = 128
V_ONES = 64
LOG2E = 1.4426950408889634
VMEM_LIMIT = 56 * 1024 * 1024

_OFF = {}
_o = 0
for _n, _w in (("a_x", 256), ("a_gate", 256), ("b_q", 256), ("k_c", 64), ("v_c", 64), ("k_s", 64), ("v_s", 64),
               ("k_w", 64), ("v_w", 64), ("b_gate", 12), ("c_q", 128), ("c_k", 128), ("c_v", 256), ("c_glr", 16),
               ("c_og", 256), ("d_cq", 192), ("d_ckv", 128), ("d_kr", 32)):
    _OFF[_n] = (_o, _w)
    _o += _w
IN_COLS = _o
_LAYOUT_A = (("a_x", 256), ("a_gate", 256))
_LAYOUT_B = (("b_q", 256), ("k_s", 64), ("k_c", 64), ("k_w", 64), (None, 64),
             ("v_c", 64), ("v_s", 64), ("v_w", 64), ("b_gate", 12), (None, 52))
_LAYOUT_C = (("c_q", 128), ("c_k", 128), ("c_v", 256), ("c_og", 256), ("c_glr", 16), (None, 112))
_LAYOUT_D = (("d_cq", 192), (None, 64), ("d_ckv", 128), (None, 64), ("d_kr", 32), (None, 32))
WA, WB, WC, WD = 512, 768, 896, 512


def _cparams(sem):
    return pltpu.CompilerParams(dimension_semantics=sem, vmem_limit_bytes=VMEM_LIMIT)


def _dot(a, b):
    return jnp.dot(a, b, preferred_element_type=F32)


def _dot_nt(a, b):
    return lax.dot_general(a, b, (((1,), (1,)), ((), ())), preferred_element_type=F32)


def _dot_tn(a, b):
    return lax.dot_general(a, b, (((0,), (0,)), ((), ())), preferred_element_type=F32)


def _split2(x):
    hi = x.astype(BF16)
    return hi, (x - hi.astype(F32)).astype(BF16)


def _dot_split(a, b, lhs_split=False):
    if lhs_split:
        hi, lo = _split2(a)
        return _dot(hi, b) + _dot(lo, b)
    hi, lo = _split2(b)
    return _dot(a, hi) + _dot(a, lo)


def _dot_x3(a, b):
    a_hi, a_lo = _split2(a)
    b_hi, b_lo = _split2(b)
    return _dot(a_hi, b_hi) + (_dot(a_hi, b_lo) + _dot(a_lo, b_hi))


def _rms(x, width):
    return x * lax.rsqrt(jnp.sum(x * x, axis=-1, keepdims=True) / width + NORM_EPS)


def _inproj_kernel(x_ref, g_ref, w_ref, oa_ref, ob_ref, oc_ref, od_ref):
    h = (_rms(x_ref[...], D_MODEL) * g_ref[...]).astype(BF16)
    off = 0
    for o_ref in (oa_ref, ob_ref, oc_ref, od_ref):
        w = o_ref.shape[1]
        o_ref[...] = _dot(h, w_ref[:, off:off + w])
        off += w


def _inproj(x2, gain, w_p, tm=512):
    T = x2.shape[0]
    ntot = w_p.shape[1]
    return pl.pallas_call(
        _inproj_kernel,
        grid=(T // tm,),
        in_specs=[pl.BlockSpec((tm, D_MODEL), lambda i: (i, 0)),
                  pl.BlockSpec((1, D_MODEL), lambda i: (0, 0)),
                  pl.BlockSpec((D_MODEL, ntot), lambda i: (0, 0))],
        out_specs=[pl.BlockSpec((tm, w), lambda i: (i, 0)) for w in (WA, WB, WC, WD)],
        out_shape=[jax.ShapeDtypeStruct((T, w), F32) for w in (WA, WB, WC, WD)],
        compiler_params=_cparams(("parallel",)),
        name="inproj",
    )(x2, gain.reshape(1, D_MODEL), w_p)


def _rglru_kernel(pa_ref, cw_ref, cb_ref, wg_ref, bg_ref, lam_ref, o_ref, xbuf, a_s, u_s, h_s, hlast):
    ts = pa_ref.shape[1]
    W = GROUP_WIDTH

    @pl.when(pl.program_id(1) == 0)
    def _():
        xbuf[0:8, :] = jnp.zeros((8, W), F32)
        hlast[...] = jnp.zeros_like(hlast)

    xbuf[8:8 + ts, :] = pa_ref[0, :, 0:W]
    xc = cb_ref[...]
    for k in range(CONV_WIDTH):
        lo = 8 - (CONV_WIDTH - 1) + k
        xc = xc + cw_ref[k:k + 1, :] * xbuf[lo:lo + ts, :]
    xbuf[0:8, :] = xbuf[ts:ts + 8, :]
    gates = _dot(xc.astype(BF16), wg_ref[...]) + bg_ref[...]
    r = jax.nn.sigmoid(gates[:, 0:W])
    i = jax.nn.sigmoid(gates[:, W:2 * W])
    log_a = (-LRU_C) * r * jax.nn.softplus(-lam_ref[...])
    a = jnp.exp(log_a)
    u = jnp.sqrt(-jnp.tanh(log_a) * (a * a + 1.0)) * (i * xc)
    a_s[...] = a
    u_s[...] = u

    def body(t, h):
        h = a_s[pl.ds(t, 1), :] * h + u_s[pl.ds(t, 1), :]
        h_s[pl.ds(t, 1), :] = h
        return h

    hlast[...] = lax.fori_loop(0, ts, body, hlast[...], unroll=8)
    o_ref[0] = h_s[...] * jax.nn.gelu(pa_ref[0, :, W:2 * W])


def _rglru(pa, conv_w, conv_b, wg, bg, lam, ts=512):
    B, S, _ = pa.shape
    W = GROUP_WIDTH
    full = lambda shape: pl.BlockSpec(shape, lambda b, j: (0,) * len(shape))
    return pl.pallas_call(
        _rglru_kernel,
        grid=(B, S // ts),
        in_specs=[pl.BlockSpec((1, ts, WA), lambda b, j: (b, j, 0)),
                  full((CONV_WIDTH, W)), full((1, W)), full((W, 2 * W)), full((1, 2 * W)), full((1, W))],
        out_specs=pl.BlockSpec((1, ts, W), lambda b, j: (b, j, 0)),
        out_shape=jax.ShapeDtypeStruct((B, S, W), F32),
        scratch_shapes=[pltpu.VMEM((ts + 8, W), F32), pltpu.VMEM((ts, W), F32), pltpu.VMEM((ts, W), F32),
                        pltpu.VMEM((ts, W), F32), pltpu.VMEM((1, W), F32)],
        compiler_params=_cparams(("parallel", "arbitrary")),
        name="rglru",
    )(pa, conv_w, conv_b.reshape(1, W), wg, bg, lam.reshape(1, W))


def _gla_kernel(pc_ref, wg2_ref, bg2_ref, gn_ref, lcs_ref, hsum_ref, gmean_ref, smask_ref, o_ref, st_ref):
    TT, SUB = GLA_TILE, GLA_SUB
    NB = TT // SUB
    KW = GLA_HEADS * GLA_DK
    VW = GLA_HEADS * GLA_DV

    @pl.when(pl.program_id(1) == 0)
    def _():
        st_ref[...] = jnp.zeros_like(st_ref)

    q = pc_ref[0, :, 0:KW] * (GLA_DK ** -0.5)
    k = pc_ref[0, :, KW:2 * KW]
    v = pc_ref[0, :, 2 * KW:2 * KW + VW]
    og = pc_ref[0, :, 2 * KW + VW:2 * KW + 2 * VW]
    glr = pc_ref[0, :, 2 * KW + 2 * VW:2 * KW + 2 * VW + LANES]
    log_a = jax.nn.log_sigmoid(_dot_x3(glr, wg2_ref[...]) + bg2_ref[...]) * (1.0 / GLA_TAU)
    cums = _dot_split(lcs_ref[...], log_a)
    b = cums[0:TT]
    bl = cums[TT:2 * TT]

    q3 = q.reshape(NB, SUB, KW)
    k3 = k.reshape(NB, SUB, KW)
    b3 = b.reshape(NB, SUB, KW)
    v3 = v.reshape(NB, SUB, VW)
    row = lax.broadcasted_iota(jnp.int32, (NB, SUB, KW), 1)
    terms = []
    for j in range(SUB):
        kj = jnp.broadcast_to(k3[:, j:j + 1, :], (NB, SUB, KW))
        bj = jnp.broadcast_to(b3[:, j:j + 1, :], (NB, SUB, KW))
        e = q3 * kj * jnp.exp(jnp.where(row >= j, b3 - bj, NEG_INF))
        terms.append(e.reshape(TT, KW))
    e_all = jnp.concatenate(terms, axis=0)
    attn = _dot(e_all.astype(BF16), hsum_ref[...])
    o = jnp.zeros((NB, SUB, VW), F32)
    for j in range(SUB):
        vj = jnp.broadcast_to(v3[:, j:j + 1, :], (NB, SUB, VW))
        o = o + attn[j * TT:(j + 1) * TT, :].reshape(NB, SUB, VW) * vj
    o = o.reshape(TT, VW)

    qd = (q * jnp.exp(b)).astype(BF16)
    kd = (k * jnp.exp(bl - b)).astype(BF16)
    dec = jnp.exp(bl)
    vb = v.astype(BF16)
    blocks = [slice(n * SUB, (n + 1) * SUB) for n in range(NB)]
    upds = [_dot_tn(vb[rows], kd[rows]) * smask_ref[...] for rows in blocks]
    st = st_ref[...]
    sts = []
    for n in range(NB):
        sts.append(st.astype(BF16))
        st = st * dec[n * SUB:n * SUB + 1, :] + upds[n]
    st_ref[...] = st
    o = o + jnp.concatenate([_dot_nt(qd[rows], s_n) for rows, s_n in zip(blocks, sts)], axis=0)

    ms = _dot_split(o * o, gmean_ref[...], lhs_split=True)
    o_ref[0] = o * lax.rsqrt(ms + NORM_EPS) * gn_ref[...] * jax.nn.silu(og)


def _gla_consts():
    TT, SUB = GLA_TILE, GLA_SUB
    KW, VW = GLA_HEADS * GLA_DK, GLA_HEADS * GLA_DV
    i = np.arange(TT)
    same = (i[:, None] // SUB) == (i[None, :] // SUB)
    lcum = (same & (i[None, :] <= i[:, None])).astype(np.float32)
    lsum = same.astype(np.float32)
    hk = np.arange(KW) // GLA_DK
    hv = np.arange(VW) // GLA_DV
    hsum = (hk[:, None] == hv[None, :]).astype(np.float32)
    gmean = (hv[:, None] == hv[None, :]).astype(np.float32) / GLA_DV
    smask = (hv[:, None] == hk[None, :]).astype(np.float32)
    return (jnp.asarray(np.concatenate([lcum, lsum], axis=0), dtype=BF16), jnp.asarray(hsum, dtype=BF16),
            jnp.asarray(gmean, dtype=BF16), jnp.asarray(smask))


def _gla(pc, wg2_p, bg2, gn_t):
    B, S, _ = pc.shape
    TT = GLA_TILE
    KW, VW = GLA_HEADS * GLA_DK, GLA_HEADS * GLA_DV
    lcs, hsum, gmean, smask = _gla_consts()
    full = lambda shape: pl.BlockSpec(shape, lambda b, j: (0,) * len(shape))
    return pl.pallas_call(
        _gla_kernel,
        grid=(B, S // TT),
        in_specs=[pl.BlockSpec((1, TT, WC), lambda b, j: (b, j, 0)),
                  full((LANES, KW)), full((1, KW)), full((1, VW)), full((2 * TT, TT)),
                  full((KW, VW)), full((VW, VW)), full((VW, KW))],
        out_specs=pl.BlockSpec((1, TT, VW), lambda b, j: (b, j, 0)),
        out_shape=jax.ShapeDtypeStruct((B, S, VW), F32),
        scratch_shapes=[pltpu.VMEM((VW, KW), F32)],
        compiler_params=_cparams(("parallel", "arbitrary")),
        name="gla",
    )(pc, wg2_p, bg2.reshape(1, KW), gn_t.reshape(1, VW), lcs, hsum, gmean, smask)


def _rope_lanes(x, cos, sin_signed, half):
    lane = lax.broadcasted_iota(jnp.int32, cos.shape, 1)
    lo = (lane % (2 * half)) < half
    outs = []
    for c in range(x.shape[1] // LANES):
        xs = x[:, c * LANES:(c + 1) * LANES]
        rot = jnp.where(lo, pltpu.roll(xs, LANES - half, 1), pltpu.roll(xs, half, 1))
        outs.append(xs * cos + rot * sin_signed)
    return outs[0] if len(outs) == 1 else jnp.concatenate(outs, axis=1)


def _rope_table_kernel(pos_ref, invf_ref, cos_ref, sin_ref, *, half):
    ang = pos_ref[0].astype(F32) * invf_ref[...]
    lane = lax.broadcasted_iota(jnp.int32, ang.shape, 1)
    sin = jnp.sin(ang)
    cos_ref[0] = jnp.cos(ang)
    sin_ref[0] = jnp.where((lane % (2 * half)) < half, -sin, sin)


def _rope_tables(pos3, d, lead_zeros=0, tail_zeros=0, tm=1024):
    B, S, _ = pos3.shape
    inv = ROPE_THETA ** (-jnp.arange(0, d, 2, dtype=F32) / d)
    reps = (LANES - lead_zeros - tail_zeros) // d
    invf = jnp.concatenate([jnp.zeros((lead_zeros,), F32), jnp.tile(jnp.concatenate([inv, inv]), reps),
                            jnp.zeros((tail_zeros,), F32)]).reshape(1, LANES)
    spec = pl.BlockSpec((1, tm, LANES), lambda b, j: (b, j, 0))
    shape = jax.ShapeDtypeStruct((B, S, LANES), F32)
    return pl.pallas_call(
        functools.partial(_rope_table_kernel, half=d // 2),
        grid=(B, S // tm),
        in_specs=[pl.BlockSpec((1, tm, 1), lambda b, j: (b, j, 0)), pl.BlockSpec((1, LANES), lambda b, j: (0, 0))],
        out_specs=[spec, spec],
        out_shape=[shape, shape],
        compiler_params=_cparams(("parallel", "parallel")),
        name="rope_tables",
    )(pos3, invf)


def _mla_proj_kernel(pd_ref, cos_ref, sin_ref, qn_ref, kvn_ref, wq_ref, wkv_ref, qt_ref, k_ref, vt_ref):
    H = MLA_HEADS
    tm = pd_ref.shape[1]
    cos, sin = cos_ref[0], sin_ref[0]
    cq = (_rms(pd_ref[0, :, 0:256], MLA_Q_RANK) * qn_ref[...]).astype(BF16)
    ckv = (_rms(pd_ref[0, :, 256:384], MLA_KV_RANK) * kvn_ref[...]).astype(BF16)
    kr = _rope_lanes(pd_ref[0, :, 384:512], cos, sin, MLA_ROPE_DIM // 2)
    q_all = _dot(cq, wq_ref[...])
    kv_all = _dot(ckv, wkv_ref[...])
    low = lax.broadcasted_iota(jnp.int32, (tm, LANES), 1) < MLA_V_DIM
    for h in range(H):
        qh = _rope_lanes(q_all[:, h * LANES:(h + 1) * LANES], cos, sin, MLA_ROPE_DIM // 2)
        qt_ref[0, h, 0] = (qh * (MLA_QK_DIM ** -0.5 * LOG2E)).T.astype(BF16)
        k_ref[0, h] = (kv_all[:, h * LANES:(h + 1) * LANES] + kr).astype(BF16)
        vt_ref[0, h, 0] = jnp.where(low, kv_all[:, (H + h) * LANES:(H + h + 1) * LANES], 1.0).T.astype(BF16)


def _mla_proj(pd, cos, sin, qn_p, kvn, wq_p, wkv_p):
    B, S, _ = pd.shape
    H = MLA_HEADS
    tm = MLA_TK
    assert MLA_TQ == tm
    full = lambda shape: pl.BlockSpec(shape, lambda b, j: (0,) * len(shape))
    tspec = pl.BlockSpec((1, H, 1, LANES, tm), lambda b, j: (b, 0, j, 0, 0))
    tshape = jax.ShapeDtypeStruct((B, H, S // tm, LANES, tm), BF16)
    return pl.pallas_call(
        _mla_proj_kernel,
        grid=(B, S // tm),
        in_specs=[pl.BlockSpec((1, tm, WD), lambda b, j: (b, j, 0)),
                  pl.BlockSpec((1, tm, LANES), lambda b, j: (b, j, 0)),
                  pl.BlockSpec((1, tm, LANES), lambda b, j: (b, j, 0)),
                  full((1, 256)), full((1, LANES)), full((256, H * LANES)), full((LANES, 2 * H * LANES))],
        out_specs=[tspec, pl.BlockSpec((1, H, tm, LANES), lambda b, j: (b, 0, j, 0)), tspec],
        out_shape=[tshape, jax.ShapeDtypeStruct((B, H, S, LANES), BF16), tshape],
        compiler_params=_cparams(("parallel", "parallel")),
        name="mla_proj",
    )(pd, cos, sin, qn_p, kvn.reshape(1, LANES), wq_p, wkv_p)


def _softmax_update(s_refs, vts, m_refs, acc_refs, mask):
    ss = [r[...] for r in s_refs]
    if mask is not None:
        ss = [jnp.where(mask, s, NEG_INF) for s in ss]
    m_prevs = [r[...] for r in m_refs]
    m_news = [jnp.maximum(mp, jnp.max(s, axis=0, keepdims=True)) for mp, s in zip(m_prevs, ss)]
    ps = [jnp.exp2(s - mn).astype(BF16) for s, mn in zip(ss, m_news)]
    for vt1, m_ref, acc_ref, p, mp, mn in zip(vts, m_refs, acc_refs, ps, m_prevs, m_news):
        acc_ref[...] = jnp.exp2(mp - mn) * acc_ref[...] + _dot(vt1, p)
        m_ref[...] = mn


def _flash_loop(n_full, scores, values, mask, buf_a, buf_b, m_refs, acc_refs):
    for r in m_refs:
        r[...] = jnp.full(r.shape, NEG_INF, F32)
    for r in acc_refs:
        r[...] = jnp.zeros(r.shape, F32)

    def fill(bufs, j):
        for r, sc in zip(bufs, scores(j)):
            r[...] = sc

    fill(buf_a, 0)

    def body(jj, carry):
        j = 2 * jj
        fill(buf_b, j + 1)
        _softmax_update(buf_a, values(j), m_refs, acc_refs, None)
        fill(buf_a, j + 2)
        _softmax_update(buf_b, values(j + 1), m_refs, acc_refs, None)
        return carry

    lax.fori_loop(0, n_full // 2, body, 0)

    @pl.when(n_full % 2 == 1)
    def _():
        fill(buf_b, n_full)
        _softmax_update(buf_a, values(n_full - 1), m_refs, acc_refs, None)
        _softmax_update(buf_b, values(n_full), m_refs, acc_refs, mask)

    @pl.when(n_full % 2 == 0)
    def _():
        _softmax_update(buf_a, values(n_full), m_refs, acc_refs, mask)


def _normalize(acc):
    return acc / jnp.maximum(acc[V_ONES:V_ONES + 1, :], 1e-30)


def _mla_attn_kernel(qt_ref, k_ref, vt_ref, o_ref, *scratch):
    tq, tk = MLA_TQ, MLA_TK
    H = MLA_HEADS
    m_refs, acc_refs, buf_a, buf_b = (scratch[i * H:(i + 1) * H] for i in range(4))
    s0 = pl.program_id(1) * tq
    n_full = s0 // tk

    def scores(j):
        off = pl.multiple_of(j * tk, tk)
        return [_dot(k_ref[0, h, pl.ds(off, tk), :], qt_ref[0, h, 0]) for h in range(H)]

    def values(j):
        return [vt_ref[0, h, j] for h in range(H)]

    kpos = n_full * tk + lax.broadcasted_iota(jnp.int32, (tk, tq), 0)
    t = s0 + lax.broadcasted_iota(jnp.int32, (tk, tq), 1)
    _flash_loop(n_full, scores, values, kpos <= t, buf_a, buf_b, m_refs, acc_refs)
    o_t = jnp.concatenate([_normalize(acc_refs[h][...])[0:MLA_V_DIM, :] for h in range(H)], axis=0)
    o_ref[0] = o_t.T


def _mla_attn(qt, k, vt):
    B, H, S, _ = k.shape
    tq, tk = MLA_TQ, MLA_TK
    return pl.pallas_call(
        _mla_attn_kernel,
        grid=(B, S // tq),
        in_specs=[pl.BlockSpec((1, H, 1, LANES, tq), lambda b, i: (b, 0, i, 0, 0)),
                  pl.BlockSpec((1, H, S, LANES), lambda b, i: (b, 0, 0, 0)),
                  pl.BlockSpec((1, H, S // tk, LANES, tk), lambda b, i: (b, 0, 0, 0, 0))],
        out_specs=pl.BlockSpec((1, tq, H * MLA_V_DIM), lambda b, i: (b, i, 0)),
        out_shape=jax.ShapeDtypeStruct((B, S, H * MLA_V_DIM), F32),
        scratch_shapes=([pltpu.VMEM((1, tq), F32)] * H + [pltpu.VMEM((LANES, tq), F32)] * H
                        + [pltpu.VMEM((tk, tq), F32)] * (2 * H)),
        compiler_params=_cparams(("parallel", "arbitrary")),
        name="mla_attn",
    )(qt, k, vt)


def _nsa_proj_kernel(pb_ref, cos_ref, sin_ref, qt_ref, ksa_ref, vst_ref, kw_ref, vwt_ref, kcvc_ref):
    tm = pb_ref.shape[1]
    H, TQ = NSA_HEADS, NSA_TQ
    r = _rope_lanes(pb_ref[0, :, 0:512], cos_ref[0], sin_ref[0], NSA_HEAD_DIM // 2)
    nr0 = pb_ref[0, :, 512:640]
    nr1 = pb_ref[0, :, 640:768]
    lane = lax.broadcasted_iota(jnp.int32, (tm, LANES), 1)
    low = lane < NSA_HEAD_DIM
    scale = NSA_HEAD_DIM ** -0.5 * LOG2E
    for h in range(H):
        seg = r[:, (h // 2) * LANES:(h // 2 + 1) * LANES]
        if h % 2:
            seg = pltpu.roll(seg, NSA_HEAD_DIM, 1)
        qh_t = jnp.where(low, seg * scale, 0.0).T.astype(BF16)
        for c in range(tm // TQ):
            qt_ref[0, c, :, h * TQ:(h + 1) * TQ] = qh_t[:, c * TQ:(c + 1) * TQ]
    kseg = r[:, 256:384]
    blk = lax.shift_right_logical(lax.broadcasted_iota(jnp.int32, (tm, LANES), 0) + pl.program_id(1) * tm, 6)
    ksa_ref[0, :, 0:LANES] = jnp.where(low, kseg, 0.0).astype(BF16)
    ksa_ref[0, :, LANES:2 * LANES] = jnp.where(blk == lane, 1.0, 0.0).astype(BF16)
    kw_ref[0] = r[:, 384:512].astype(BF16)
    vst_ref[0, 0] = jnp.where(low, pltpu.roll(nr0, NSA_HEAD_DIM, 1), 1.0).T.astype(BF16)
    vw_t = jnp.where(low, nr1, 1.0).T.astype(BF16)
    for c in range(tm // LANES):
        vwt_ref[0, c] = vw_t[:, c * LANES:(c + 1) * LANES]
    kcvc_ref[0] = jnp.where(low, pltpu.roll(kseg, NSA_HEAD_DIM, 1), pltpu.roll(nr0, NSA_HEAD_DIM, 1))


def _nsa_proj(pb, cos, sin):
    B, S, _ = pb.shape
    H, TQ = NSA_HEADS, NSA_TQ
    tm = NSA_TK
    tok = lambda w: pl.BlockSpec((1, tm, w), lambda b, j: (b, j, 0))
    return pl.pallas_call(
        _nsa_proj_kernel,
        grid=(B, S // tm),
        in_specs=[tok(WB), tok(LANES), tok(LANES)],
        out_specs=[pl.BlockSpec((1, tm // TQ, LANES, H * TQ), lambda b, j: (b, j, 0, 0)),
                   tok(2 * LANES),
                   pl.BlockSpec((1, 1, LANES, tm), lambda b, j: (b, j, 0, 0)),
                   tok(LANES),
                   pl.BlockSpec((1, tm // LANES, LANES, LANES), lambda b, j: (b, j, 0, 0)),
                   tok(LANES)],
        out_shape=[jax.ShapeDtypeStruct((B, S // TQ, LANES, H * TQ), BF16),
                   jax.ShapeDtypeStruct((B, S, 2 * LANES), BF16),
                   jax.ShapeDtypeStruct((B, S // tm, LANES, tm), BF16),
                   jax.ShapeDtypeStruct((B, S, LANES), BF16),
                   jax.ShapeDtypeStruct((B, S // LANES, LANES, LANES), BF16),
                   jax.ShapeDtypeStruct((B, S, LANES), F32)],
        compiler_params=_cparams(("parallel", "parallel")),
        name="nsa_proj",
    )(pb, cos, sin)


def _nsa_cmp_kernel(t_ref, ptop_ref, pbot_ref, wtop_ref, wbot_ref, b1_ref, w2k_ref, w2v_ref, kc_ref, vct_ref, sh_ref):
    n = t_ref.shape[1]
    t = t_ref[0]
    top = _dot((t + ptop_ref[...]).astype(BF16), wtop_ref[...])
    sh_ref[0:n, :] = _dot((t + pbot_ref[...]).astype(BF16), wbot_ref[...])
    sh_ref[n:n + 8, :] = jnp.zeros((8, sh_ref.shape[1]), F32)
    hid = jax.nn.gelu(top + sh_ref[pl.ds(1, n), :] + b1_ref[...])
    kc_ref[0] = _dot(hid[:, 0:CMP_HIDDEN].astype(BF16), w2k_ref[...]).astype(BF16)
    vc = _dot(hid[:, CMP_HIDDEN:2 * CMP_HIDDEN].astype(BF16), w2v_ref[...])
    low = lax.broadcasted_iota(jnp.int32, vc.shape, 1) < NSA_HEAD_DIM
    vct_ref[0] = jnp.where(low, vc, 1.0).T.astype(BF16)


def _nsa_compress(t16, ptop, pbot, wtop, wbot, b1, w2k, w2v):
    B, n, F = t16.shape
    full = lambda shape: pl.BlockSpec(shape, lambda b: (0,) * len(shape))
    return pl.pallas_call(
        _nsa_cmp_kernel,
        grid=(B,),
        in_specs=[pl.BlockSpec((1, n, F), lambda b: (b, 0, 0)), full((1, F)), full((1, F)),
                  full((F, 2 * CMP_HIDDEN)), full((F, 2 * CMP_HIDDEN)), full((1, 2 * CMP_HIDDEN)),
                  full((CMP_HIDDEN, LANES)), full((CMP_HIDDEN, LANES))],
        out_specs=[pl.BlockSpec((1, n, LANES), lambda b: (b, 0, 0)), pl.BlockSpec((1, LANES, n), lambda b: (b, 0, 0))],
        out_shape=[jax.ShapeDtypeStruct((B, n, LANES), BF16), jax.ShapeDtypeStruct((B, LANES, n), BF16)],
        scratch_shapes=[pltpu.VMEM((n + 8, 2 * CMP_HIDDEN), F32)],
        compiler_params=_cparams(("parallel",)),
        name="nsa_compress",
    )(t16, ptop, pbot, wtop, wbot, b1, w2k, w2v)


def _nsa_attn_kernel(qt_ref, g_ref, kc_ref, vct_ref, ovl_ref, ksa_ref, vst_ref, kw_ref, vwt_ref, o_ref, *scratch):
    H, TQ, TK = NSA_HEADS, NSA_TQ, NSA_TK
    M = H * TQ
    NB = MAX_SEL_BLOCKS
    s0 = pl.program_id(1) * TQ
    ncmp = kc_ref.shape[1]
    qt = qt_ref[0, 0]
    t_row = s0 + lax.broadcasted_iota(jnp.int32, (1, M), 1) % TQ

    cend = lax.broadcasted_iota(jnp.int32, (ncmp, M), 0) * CMP_STRIDE + (CMP_LEN - 1)
    s_c = jnp.where(cend <= t_row, _dot(kc_ref[0], qt), NEG_INF)
    m_c = jnp.max(s_c, axis=0, keepdims=True)
    e_c = jnp.exp2(s_c - m_c)
    a_c = _dot(vct_ref[0], e_c.astype(BF16))
    inv_c = jnp.where(m_c > 0.5 * NEG_INF, 1.0 / jnp.maximum(a_c[V_ONES:V_ONES + 1, :], 1e-30), 0.0)
    o_c = a_c * inv_c
    p_c = e_c * inv_c

    WK = WINDOW + TQ
    w0 = pl.multiple_of(jnp.maximum(s0 - WINDOW, 0), TQ)
    dist = t_row - (w0 + lax.broadcasted_iota(jnp.int32, (WK, M), 0))
    in_window = lax.bitcast_convert_type(dist, jnp.uint32) < jnp.uint32(WINDOW)
    s_w = jnp.where(in_window, _dot(kw_ref[0, pl.ds(w0, WK), :], qt), NEG_INF)
    e_w = jnp.exp2(s_w - jnp.max(s_w, axis=0, keepdims=True)).astype(BF16)
    a_w = _dot(vwt_ref[0, w0 // LANES], e_w[0:LANES])
    for c in range(1, WK // LANES):
        a_w = a_w + _dot(vwt_ref[0, w0 // LANES + c], e_w[c * LANES:(c + 1) * LANES])
    o_w = _normalize(a_w)

    psum = p_c[:, 0:TQ] + p_c[:, TQ:2 * TQ] + p_c[:, 2 * TQ:3 * TQ] + p_c[:, 3 * TQ:4 * TQ]
    p_hi = psum.astype(BF16)
    p_lo = (psum - p_hi.astype(F32)).astype(BF16)
    imp = _dot(ovl_ref[...], p_hi) + _dot(ovl_ref[...], p_lo)
    jb = lax.broadcasted_iota(jnp.int32, (NB, TQ), 0)
    tq_l = s0 + lax.broadcasted_iota(jnp.int32, (NB, TQ), 1)
    cur = lax.shift_right_logical(tq_l, 6)
    forced = (jb == 0) | (jb == cur) | (jb == cur - 1)
    score = jnp.where(forced, -3e38, jnp.where(jb * SEL_LEN <= tq_l, imp, NEG_INF))
    jbf = jb.astype(F32)
    sel = jnp.where(forced, 1.0, 0.0)
    for _ in range(SEL_TOPN - 3):
        best = jnp.max(score, axis=0, keepdims=True)
        first = jnp.min(jnp.where(score == best, jbf, float(NB)), axis=0, keepdims=True)
        hit = jbf == first
        sel = jnp.where(hit, 1.0, sel)
        score = jnp.where(hit, -3e38, score)
    bias = ((sel - 1.0) * (-NEG_INF)).astype(BF16)
    q_aug = jnp.concatenate([qt, jnp.concatenate([bias] * H, axis=1)], axis=0)

    NC = NSA_CHAINS
    CW = M // NC
    m_refs, acc_refs, buf_a, buf_b = (scratch[i * NC:(i + 1) * NC] for i in range(4))
    n_full = s0 // TK
    q_cols = [q_aug[:, c * CW:(c + 1) * CW] for c in range(NC)]

    def scores(j):
        k = ksa_ref[0, pl.ds(pl.multiple_of(j * TK, TK), TK), :]
        return [_dot(k, qc) for qc in q_cols]

    def values(j):
        return [vst_ref[0, j]] * NC

    kpos = n_full * TK + lax.broadcasted_iota(jnp.int32, (TK, CW), 0)
    _flash_loop(n_full, scores, values, kpos <= t_row[:, 0:CW], buf_a, buf_b, m_refs, acc_refs)
    o_s = jnp.concatenate([_normalize(acc_refs[c][...]) for c in range(NC)], axis=1)

    g = jax.nn.sigmoid(g_ref[0]).T
    outs = []
    for h in range(H):
        cols = slice(h * TQ, (h + 1) * TQ)
        r0 = NSA_HEAD_DIM + 3 * h
        o_h = (g[r0:r0 + 1, :] * o_c[0:NSA_HEAD_DIM, cols] + g[r0 + 1:r0 + 2, :] * o_s[0:NSA_HEAD_DIM, cols]
               + g[r0 + 2:r0 + 3, :] * o_w[0:NSA_HEAD_DIM, cols])
        outs.append(o_h)
    o_ref[0] = jnp.concatenate(outs, axis=0).T


def _nsa_attn(qt, pb, kc, vct, ovl, ksa, vst, kw, vwt):
    B, S, _ = ksa.shape
    H, TQ, TK = NSA_HEADS, NSA_TQ, NSA_TK
    ncmp = kc.shape[1]
    per_b = lambda *shape: pl.BlockSpec((1,) + shape, lambda b, i: (b,) + (0,) * len(shape))
    return pl.pallas_call(
        _nsa_attn_kernel,
        grid=(B, S // TQ),
        in_specs=[pl.BlockSpec((1, 1, LANES, H * TQ), lambda b, i: (b, i, 0, 0)),
                  pl.BlockSpec((1, TQ, LANES), lambda b, i: (b, i, WB // LANES - 1)),
                  per_b(ncmp, LANES), per_b(LANES, ncmp),
                  pl.BlockSpec((MAX_SEL_BLOCKS, ncmp), lambda b, i: (0, 0)),
                  per_b(S, 2 * LANES), per_b(S // TK, LANES, TK), per_b(S, LANES), per_b(S // LANES, LANES, LANES)],
        out_specs=pl.BlockSpec((1, TQ, H * NSA_HEAD_DIM), lambda b, i: (b, i, 0)),
        out_shape=jax.ShapeDtypeStruct((B, S, H * NSA_HEAD_DIM), F32),
        scratch_shapes=([pltpu.VMEM((1, H * TQ // NSA_CHAINS), F32)] * NSA_CHAINS
                        + [pltpu.VMEM((LANES, H * TQ // NSA_CHAINS), F32)] * NSA_CHAINS
                        + [pltpu.VMEM((TK, H * TQ // NSA_CHAINS), F32)] * (2 * NSA_CHAINS)),
        compiler_params=_cparams(("parallel", "arbitrary")),
        name="nsa_attn",
    )(qt, pb, kc, vct, ovl, ksa, vst, kw, vwt)


def _nsa_overlap(ncmp):
    n = np.arange(ncmp)[None, :] * CMP_STRIDE
    s = np.arange(MAX_SEL_BLOCKS)[:, None] * SEL_LEN
    real = np.arange(ncmp)[None, :] < ncmp - 1
    ovl = (n < s + SEL_LEN) & (n + CMP_LEN > s) & real
    return jnp.asarray(ovl.astype(np.float32), dtype=BF16)


def _mix_ffn_kernel(ya_ref, yb_ref, yc_ref, yd_ref, x_ref, gn_ref, wo_ref, g_ref, wgu_ref, wd_ref, fn_ref, o_ref, *,
                    final_norm):
    x = x_ref[...]
    for g, y_ref in enumerate((ya_ref, yb_ref, yc_ref, yd_ref)):
        sl = slice(g * GROUP_WIDTH, (g + 1) * GROUP_WIDTH)
        y = (_rms(y_ref[...], GROUP_WIDTH) * gn_ref[:, sl]).astype(BF16)
        x = x + _dot(y, wo_ref[sl, :])
    o_ref[...] = x
    x = o_ref[...]
    h = (_rms(x, D_MODEL) * g_ref[...]).astype(BF16)
    acc = x
    for c in range(D_FF // FF_CHUNK):
        lo = c * FF_CHUNK
        gate = _dot(h, wgu_ref[:, lo:lo + FF_CHUNK])
        up = _dot(h, wgu_ref[:, D_FF + lo:D_FF + lo + FF_CHUNK])
        acc = acc + _dot((jax.nn.silu(gate) * up).astype(BF16), wd_ref[lo:lo + FF_CHUNK, :])
    if final_norm:
        acc = _rms(acc, D_MODEL) * fn_ref[...]
    o_ref[...] = acc


def _mix_ffn(ys, x2, gn, w_out_b, gain, wgu_b, wd_b, fnorm, final_norm, tm=512):
    T = x2.shape[0]
    W = GROUP_WIDTH
    row = pl.BlockSpec((1, D_MODEL), lambda i: (0, 0))
    resident = lambda shape: pl.BlockSpec(shape, lambda i: (0, 0))
    return pl.pallas_call(
        functools.partial(_mix_ffn_kernel, final_norm=final_norm),
        grid=(T // tm,),
        in_specs=[pl.BlockSpec((tm, W), lambda i: (i, 0))] * 4
        + [pl.BlockSpec((tm, D_MODEL), lambda i: (i, 0)), row, resident((D_MODEL, D_MODEL)), row,
           resident((D_MODEL, 2 * D_FF)), resident((D_FF, D_MODEL)), row],
        out_specs=pl.BlockSpec((tm, D_MODEL), lambda i: (i, 0)),
        out_shape=jax.ShapeDtypeStruct((T, D_MODEL), F32),
        compiler_params=_cparams(("parallel",)),
        name="mix_ffn",
    )(*ys, x2, gn.reshape(1, D_MODEL), w_out_b, gain.reshape(1, D_MODEL), wgu_b, wd_b, fnorm.reshape(1, D_MODEL))


def _gather_cols(w, layout):
    parts = []
    for name, width in layout:
        if name is None:
            parts.append(jnp.zeros((w.shape[0], width), w.dtype))
        else:
            o, n = _OFF[name]
            assert n == width
            parts.append(w[:, o:o + n])
    return jnp.concatenate(parts, axis=1)


def _block_diag(w):
    h, d, _ = w.shape
    out = jnp.zeros((h * d, h * d), w.dtype)
    for i in range(h):
        out = out.at[i * d:(i + 1) * d, i * d:(i + 1) * d].set(w[i])
    return out


def _pad_to(a, shape):
    return jnp.pad(a, [(0, s - d) for s, d in zip(shape, a.shape)])


def _mla_weights(w_uq, w_ukv):
    H = MLA_HEADS
    wq = w_uq.reshape(MLA_Q_RANK, H, MLA_QK_DIM)
    wq = _pad_to(wq, (256, H, LANES)).reshape(256, H * LANES)
    wkv = w_ukv.reshape(MLA_KV_RANK, H, MLA_NOPE_DIM + MLA_V_DIM)
    wk = _pad_to(wkv[:, :, :MLA_NOPE_DIM], (MLA_KV_RANK, H, LANES)).reshape(MLA_KV_RANK, H * LANES)
    wv = _pad_to(wkv[:, :, MLA_NOPE_DIM:], (MLA_KV_RANK, H, LANES)).reshape(MLA_KV_RANK, H * LANES)
    return wq.astype(BF16), jnp.concatenate([wk, wv], axis=1).astype(BF16)


def _cmp_weights(cmp_pos, cmp_w1, cmp_b1, cmp_w2):
    half = CMP_LEN // 2
    Dh = NSA_HEAD_DIM
    w1 = cmp_w1.reshape(2, CMP_LEN, Dh, CMP_HIDDEN)

    def rows(part):
        wk = w1[0, part * half:(part + 1) * half]
        wv = w1[1, part * half:(part + 1) * half]
        z = jnp.zeros_like(wk)
        top = jnp.concatenate([wk, z], axis=-1)
        bot = jnp.concatenate([z, wv], axis=-1)
        return jnp.concatenate([top, bot], axis=1).reshape(half * 2 * Dh, 2 * CMP_HIDDEN)

    def pos_row(part):
        p = jnp.concatenate([cmp_pos[0, part * half:(part + 1) * half], cmp_pos[1, part * half:(part + 1) * half]],
                            axis=-1)
        return p.reshape(1, half * 2 * Dh)

    b1 = jnp.concatenate([cmp_b1[0], cmp_b1[1]]).reshape(1, 2 * CMP_HIDDEN)
    w2k = _pad_to(cmp_w2[0], (CMP_HIDDEN, LANES)).astype(BF16)
    w2v = _pad_to(cmp_w2[1], (CMP_HIDDEN, LANES)).astype(BF16)
    return pos_row(0), pos_row(1), rows(0).astype(BF16), rows(1).astype(BF16), b1, w2k, w2v


def kernel(x, positions, norm_mix, w_in, conv_w, conv_b, lru_wa, lru_ba, lru_wx, lru_bx, lru_lambda, cmp_pos, cmp_w1, cmp_b1, cmp_w2, gla_wg2, gla_bg2, gla_norm, mla_q_norm, mla_kv_norm, mla_w_uq, mla_w_ukv, group_norm, w_out, norm_ffn, w_gate_up, w_down, final_norm):
    B, S, D = x.shape
    assert D == D_MODEL and S % (CMP_STRIDE * 8) == 0 and S // SEL_LEN <= MAX_SEL_BLOCKS and S >= WINDOW + NSA_TQ
    depth = w_in.shape[0]
    T = B * S
    pos3 = positions.reshape(B, S, 1)
    ncmp = S // CMP_STRIDE
    ovl = _nsa_overlap(ncmp)
    cos_n, sin_n = _rope_tables(pos3, NSA_HEAD_DIM)
    cos_m, sin_m = _rope_tables(pos3, MLA_ROPE_DIM, lead_zeros=MLA_NOPE_DIM, tail_zeros=LANES - MLA_QK_DIM)
    x2 = x.reshape(T, D)
    for l in range(depth):
        w_p = jnp.concatenate([_gather_cols(w_in[l], lay) for lay in (_LAYOUT_A, _LAYOUT_B, _LAYOUT_C, _LAYOUT_D)],
                              axis=1).astype(BF16)
        pa, pb, pc, pd = _inproj(x2, norm_mix[l], w_p)
        pa, pb, pc, pd = (p.reshape(B, S, -1) for p in (pa, pb, pc, pd))

        wg = jnp.concatenate([_block_diag(lru_wa[l]), _block_diag(lru_wx[l])], axis=1).astype(BF16)
        bg = jnp.concatenate([lru_ba[l], lru_bx[l]]).reshape(1, 2 * GROUP_WIDTH)
        y_a = _rglru(pa, conv_w[l], conv_b[l], wg, bg, lru_lambda[l])

        qt_n, ksa, vst, kw, vwt, kcvc = _nsa_proj(pb, cos_n, sin_n)
        t16 = kcvc.reshape(B, ncmp, CMP_STRIDE * 2 * NSA_HEAD_DIM)
        kc, vct = _nsa_compress(t16, *_cmp_weights(cmp_pos[l], cmp_w1[l], cmp_b1[l], cmp_w2[l]))
        y_b = _nsa_attn(qt_n, pb, kc, vct, ovl, ksa, vst, kw, vwt)

        wg2_p = _pad_to(gla_wg2[l], (LANES, GLA_HEADS * GLA_DK))
        y_c = _gla(pc, wg2_p, gla_bg2[l], jnp.tile(gla_norm[l], GLA_HEADS))

        wq_p, wkv_p = _mla_weights(mla_w_uq[l], mla_w_ukv[l])
        qt_m, k_m, vt_m = _mla_proj(pd, cos_m, sin_m, _pad_to(mla_q_norm[l], (256,)).reshape(1, 256), mla_kv_norm[l],
                                    wq_p, wkv_p)
        y_d = _mla_attn(qt_m, k_m, vt_m)

        ys = [y.reshape(T, GROUP_WIDTH) for y in (y_a, y_b, y_c, y_d)]
        x2 = _mix_ffn(ys, x2, group_norm[l], w_out[l].astype(BF16), norm_ffn[l], w_gate_up[l].astype(BF16),
                      w_down[l].astype(BF16), final_norm, final_norm=(l == depth - 1))
    return x2.reshape(B, S, D)
```

```python
import functools

import numpy as np
import jax
import jax.numpy as jnp
from jax import lax
from jax.experimental import pallas as pl
from jax.experimental.pallas import tpu as pltpu

F32 = jnp.float32
BF16 = jnp.bfloat16

D_MODEL = 1024
GROUP_WIDTH = 256
ROPE_THETA = 10000.0
NORM_EPS = 1e-6
NEG_INF = -1e30
FORCE_SCORE = 1e9

LRU_C = 8.0
CONV_WIDTH = 4

NSA_HEADS = 4
NSA_HEAD_DIM = 64
CMP_LEN = 32
CMP_STRIDE = 16
CMP_HIDDEN = 256
SEL_LEN = 64
SEL_TOPN = 16
WINDOW = 512
NSA_TQ = 256
NSA_TK = 512
NSA_CHAINS = 2
MAX_SEL_BLOCKS = 128

GLA_HEADS = 4
GLA_DV = 64
GLA_DK = 32
GLA_GATE_RANK = 16
GLA_TAU = 16.0
GLA_TILE = 256
GLA_SUB = 16

MLA_HEADS = 4
MLA_V_DIM = 64
MLA_NOPE_DIM = 64
MLA_ROPE_DIM = 32
MLA_QK_DIM = 96
MLA_Q_RANK = 192
MLA_KV_RANK = 128
MLA_TQ = 512
MLA_TK = 512

D_FF = 2816
FF_CHUNK = 256

LANES = 128
V_ONES = 64
LOG2E = 1.4426950408889634
VMEM_LIMIT = 56 * 1024 * 1024

_OFF = {}
_o = 0
for _n, _w in (("a_x", 256), ("a_gate", 256), ("b_q", 256), ("k_c", 64), ("v_c", 64), ("k_s", 64), ("v_s", 64),
               ("k_w", 64), ("v_w", 64), ("b_gate", 12), ("c_q", 128), ("c_k", 128), ("c_v", 256), ("c_glr", 16),
               ("c_og", 256), ("d_cq", 192), ("d_ckv", 128), ("d_kr", 32)):
    _OFF[_n] = (_o, _w)
    _o += _w
IN_COLS = _o
_LAYOUT_A = (("a_x", 256), ("a_gate", 256))
_LAYOUT_B = (("b_q", 256), ("k_s", 64), ("k_c", 64), ("k_w", 64), (None, 64),
             ("v_c", 64), ("v_s", 64), ("v_w", 64), ("b_gate", 12), (None, 52))
_LAYOUT_C = (("c_q", 128), ("c_k", 128), ("c_v", 256), ("c_og", 256), ("c_glr", 16), (None, 112))
_LAYOUT_D = (("d_cq", 192), (None, 64), ("d_ckv", 128), (None, 64), ("d_kr", 32), (None, 32))
WA, WB, WC, WD = 512, 768, 896, 512


def _cparams(sem):
    return pltpu.CompilerParams(dimension_semantics=sem, vmem_limit_bytes=VMEM_LIMIT)


def _dot(a, b):
    return jnp.dot(a, b, preferred_element_type=F32)


def _dot_nt(a, b):
    return lax.dot_general(a, b, (((1,), (1,)), ((), ())), preferred_element_type=F32)


def _dot_tn(a, b):
    return lax.dot_general(a, b, (((0,), (0,)), ((), ())), preferred_element_type=F32)


def _split2(x):
    hi = x.astype(BF16)
    return hi, (x - hi.astype(F32)).astype(BF16)


def _dot_split(a, b, lhs_split=False):
    if lhs_split:
        hi, lo = _split2(a)
        return _dot(hi, b) + _dot(lo, b)
    hi, lo = _split2(b)
    return _dot(a, hi) + _dot(a, lo)


def _dot_x3(a, b):
    a_hi, a_lo = _split2(a)
    b_hi, b_lo = _split2(b)
    return _dot(a_hi, b_hi) + (_dot(a_hi, b_lo) + _dot(a_lo, b_hi))


def _rms(x, width):
    return x * lax.rsqrt(jnp.sum(x * x, axis=-1, keepdims=True) / width + NORM_EPS)


def _inproj_kernel(x_ref, g_ref, w_ref, oa_ref, ob_ref, oc_ref, od_ref):
    h = (_rms(x_ref[...], D_MODEL) * g_ref[...]).astype(BF16)
    off = 0
    for o_ref in (oa_ref, ob_ref, oc_ref, od_ref):
        w = o_ref.shape[1]
        o_ref[...] = _dot(h, w_ref[:, off:off + w])
        off += w


def _inproj(x2, gain, w_p, tm=512):
    T = x2.shape[0]
    ntot = w_p.shape[1]
    return pl.pallas_call(
        _inproj_kernel,
        grid=(T // tm,),
        in_specs=[pl.BlockSpec((tm, D_MODEL), lambda i: (i, 0)),
                  pl.BlockSpec((1, D_MODEL), lambda i: (0, 0)),
                  pl.BlockSpec((D_MODEL, ntot), lambda i: (0, 0))],
        out_specs=[pl.BlockSpec((tm, w), lambda i: (i, 0)) for w in (WA, WB, WC, WD)],
        out_shape=[jax.ShapeDtypeStruct((T, w), F32) for w in (WA, WB, WC, WD)],
        compiler_params=_cparams(("parallel",)),
        name="inproj",
    )(x2, gain.reshape(1, D_MODEL), w_p)


def _rglru_kernel(pa_ref, cw_ref, cb_ref, wg_ref, bg_ref, lam_ref, o_ref, xbuf, a_s, u_s, h_s, hlast):
    ts = pa_ref.shape[1]
    W = GROUP_WIDTH

    @pl.when(pl.program_id(1) == 0)
    def _():
        xbuf[0:8, :] = jnp.zeros((8, W), F32)
        hlast[...] = jnp.zeros_like(hlast)

    xbuf[8:8 + ts, :] = pa_ref[0, :, 0:W]
    xc = cb_ref[...]
    for k in range(CONV_WIDTH):
        lo = 8 - (CONV_WIDTH - 1) + k
        xc = xc + cw_ref[k:k + 1, :] * xbuf[lo:lo + ts, :]
    xbuf[0:8, :] = xbuf[ts:ts + 8, :]
    gates = _dot(xc.astype(BF16), wg_ref[...]) + bg_ref[...]
    r = jax.nn.sigmoid(gates[:, 0:W])
    i = jax.nn.sigmoid(gates[:, W:2 * W])
    log_a = (-LRU_C) * r * jax.nn.softplus(-lam_ref[...])
    a = jnp.exp(log_a)
    u = jnp.sqrt(-jnp.tanh(log_a) * (a * a + 1.0)) * (i * xc)
    a_s[...] = a
    u_s[...] = u

    def body(t, h):
        h = a_s[pl.ds(t, 1), :] * h + u_s[pl.ds(t, 1), :]
        h_s[pl.ds(t, 1), :] = h
        return h

    hlast[...] = lax.fori_loop(0, ts, body, hlast[...], unroll=8)
    o_ref[0] = h_s[...] * jax.nn.gelu(pa_ref[0, :, W:2 * W])


def _rglru(pa, conv_w, conv_b, wg, bg, lam, ts=512):
    B, S, _ = pa.shape
    W = GROUP_WIDTH
    full = lambda shape: pl.BlockSpec(shape, lambda b, j: (0,) * len(shape))
    return pl.pallas_call(
        _rglru_kernel,
        grid=(B, S // ts),
        in_specs=[pl.BlockSpec((1, ts, WA), lambda b, j: (b, j, 0)),
                  full((CONV_WIDTH, W)), full((1, W)), full((W, 2 * W)), full((1, 2 * W)), full((1, W))],
        out_specs=pl.BlockSpec((1, ts, W), lambda b, j: (b, j, 0)),
        out_shape=jax.ShapeDtypeStruct((B, S, W), F32),
        scratch_shapes=[pltpu.VMEM((ts + 8, W), F32), pltpu.VMEM((ts, W), F32), pltpu.VMEM((ts, W), F32),
                        pltpu.VMEM((ts, W), F32), pltpu.VMEM((1, W), F32)],
        compiler_params=_cparams(("parallel", "arbitrary")),
        name="rglru",
    )(pa, conv_w, conv_b.reshape(1, W), wg, bg, lam.reshape(1, W))


def _gla_kernel(pc_ref, wg2_ref, bg2_ref, gn_ref, lcs_ref, hsum_ref, gmean_ref, smask_ref, o_ref, st_ref):
    TT, SUB = GLA_TILE, GLA_SUB
    NB = TT // SUB
    KW = GLA_HEADS * GLA_DK
    VW = GLA_HEADS * GLA_DV

    @pl.when(pl.program_id(1) == 0)
    def _():
        st_ref[...] = jnp.zeros_like(st_ref)

    q = pc_ref[0, :, 0:KW] * (GLA_DK ** -0.5)
    k = pc_ref[0, :, KW:2 * KW]
    v = pc_ref[0, :, 2 * KW:2 * KW + VW]
    og = pc_ref[0, :, 2 * KW + VW:2 * KW + 2 * VW]
    glr = pc_ref[0, :, 2 * KW + 2 * VW:2 * KW + 2 * VW + LANES]
    log_a = jax.nn.log_sigmoid(_dot_x3(glr, wg2_ref[...]) + bg2_ref[...]) * (1.0 / GLA_TAU)
    cums = _dot_split(lcs_ref[...], log_a)
    b = cums[0:TT]
    bl = cums[TT:2 * TT]

    q3 = q.reshape(NB, SUB, KW)
    k3 = k.reshape(NB, SUB, KW)
    b3 = b.reshape(NB, SUB, KW)
    v3 = v.reshape(NB, SUB, VW)
    row = lax.broadcasted_iota(jnp.int32, (NB, SUB, KW), 1)
    terms = []
    for j in range(SUB):
        kj = jnp.broadcast_to(k3[:, j:j + 1, :], (NB, SUB, KW))
        bj = jnp.broadcast_to(b3[:, j:j + 1, :], (NB, SUB, KW))
        e = q3 * kj * jnp.exp(jnp.where(row >= j, b3 - bj, NEG_INF))
        terms.append(e.reshape(TT, KW))
    e_all = jnp.concatenate(terms, axis=0)
    attn = _dot(e_all.astype(BF16), hsum_ref[...])
    o = jnp.zeros((NB, SUB, VW), F32)
    for j in range(SUB):
        vj = jnp.broadcast_to(v3[:, j:j + 1, :], (NB, SUB, VW))
        o = o + attn[j * TT:(j + 1) * TT, :].reshape(NB, SUB, VW) * vj
    o = o.reshape(TT, VW)

    qd = (q * jnp.exp(b)).astype(BF16)
    kd = (k * jnp.exp(bl - b)).astype(BF16)
    dec = jnp.exp(bl)
    vb = v.astype(BF16)
    blocks = [slice(n * SUB, (n + 1) * SUB) for n in range(NB)]
    upds = [_dot_tn(vb[rows], kd[rows]) * smask_ref[...] for rows in blocks]
    st = st_ref[...]
    sts = []
    for n in range(NB):
        sts.append(st.astype(BF16))
        st = st * dec[n * SUB:n * SUB + 1, :] + upds[n]
    st_ref[...] = st
    o = o + jnp.concatenate([_dot_nt(qd[rows], s_n) for rows, s_n in zip(blocks, sts)], axis=0)

    ms = _dot_split(o * o, gmean_ref[...], lhs_split=True)
    o_ref[0] = o * lax.rsqrt(ms + NORM_EPS) * gn_ref[...] * jax.nn.silu(og)


def _gla_consts():
    TT, SUB = GLA_TILE, GLA_SUB
    KW, VW = GLA_HEADS * GLA_DK, GLA_HEADS * GLA_DV
    i = np.arange(TT)
    same = (i[:, None] // SUB) == (i[None, :] // SUB)
    lcum = (same & (i[None, :] <= i[:, None])).astype(np.float32)
    lsum = same.astype(np.float32)
    hk = np.arange(KW) // GLA_DK
    hv = np.arange(VW) // GLA_DV
    hsum = (hk[:, None] == hv[None, :]).astype(np.float32)
    gmean = (hv[:, None] == hv[None, :]).astype(np.float32) / GLA_DV
    smask = (hv[:, None] == hk[None, :]).astype(np.float32)
    return (jnp.asarray(np.concatenate([lcum, lsum], axis=0), dtype=BF16), jnp.asarray(hsum, dtype=BF16),
            jnp.asarray(gmean, dtype=BF16), jnp.asarray(smask))


def _gla(pc, wg2_p, bg2, gn_t):
    B, S, _ = pc.shape
    TT = GLA_TILE
    KW, VW = GLA_HEADS * GLA_DK, GLA_HEADS * GLA_DV
    lcs, hsum, gmean, smask = _gla_consts()
    full = lambda shape: pl.BlockSpec(shape, lambda b, j: (0,) * len(shape))
    return pl.pallas_call(
        _gla_kernel,
        grid=(B, S // TT),
        in_specs=[pl.BlockSpec((1, TT, WC), lambda b, j: (b, j, 0)),
                  full((LANES, KW)), full((1, KW)), full((1, VW)), full((2 * TT, TT)),
                  full((KW, VW)), full((VW, VW)), full((VW, KW))],
        out_specs=pl.BlockSpec((1, TT, VW), lambda b, j: (b, j, 0)),
        out_shape=jax.ShapeDtypeStruct((B, S, VW), F32),
        scratch_shapes=[pltpu.VMEM((VW, KW), F32)],
        compiler_params=_cparams(("parallel", "arbitrary")),
        name="gla",
    )(pc, wg2_p, bg2.reshape(1, KW), gn_t.reshape(1, VW), lcs, hsum, gmean, smask)


def _rope_lanes(x, cos, sin_signed, half):
    lane = lax.broadcasted_iota(jnp.int32, cos.shape, 1)
    lo = (lane % (2 * half)) < half
    outs = []
    for c in range(x.shape[1] // LANES):
        xs = x[:, c * LANES:(c + 1) * LANES]
        rot = jnp.where(lo, pltpu.roll(xs, LANES - half, 1), pltpu.roll(xs, half, 1))
        outs.append(xs * cos + rot * sin_signed)
    return outs[0] if len(outs) == 1 else jnp.concatenate(outs, axis=1)


def _rope_table_kernel(pos_ref, invf_ref, cn_ref, sn_ref, cm_ref, sm_ref):
    ang = pos_ref[0].astype(F32) * invf_ref[...]
    lane = lax.broadcasted_iota(jnp.int32, ang.shape, 1)
    hn, hm = NSA_HEAD_DIM // 2, MLA_ROPE_DIM // 2
    for src, n_ref, m_ref, is_sin in ((jnp.cos(ang), cn_ref, cm_ref, False), (jnp.sin(ang), sn_ref, sm_ref, True)):
        t = jnp.where(lane < hn, src, pltpu.roll(src, hn, 1))
        t = jnp.where(lane < 2 * hn, t, pltpu.roll(t, 2 * hn, 1))
        m = jnp.where(lane < MLA_NOPE_DIM + hm, pltpu.roll(src, MLA_NOPE_DIM - hn, 1),
                      pltpu.roll(src, MLA_NOPE_DIM - hn + hm, 1))
        rope_m = (lane >= MLA_NOPE_DIM) & (lane < MLA_QK_DIM)
        if is_sin:
            n_ref[0] = jnp.where(lane % (2 * hn) < hn, -t, t)
            m_ref[0] = jnp.where(rope_m, jnp.where(lane % (2 * hm) < hm, -m, m), 0.0)
        else:
            n_ref[0] = t
            m_ref[0] = jnp.where(rope_m, m, 1.0)


def _rope_tables(pos3, tm=1024):
    B, S, _ = pos3.shape
    inv_n = ROPE_THETA ** (-jnp.arange(0, NSA_HEAD_DIM, 2, dtype=F32) / NSA_HEAD_DIM)
    inv_m = ROPE_THETA ** (-jnp.arange(0, MLA_ROPE_DIM, 2, dtype=F32) / MLA_ROPE_DIM)
    invf = jnp.concatenate([inv_n, inv_m, jnp.zeros((LANES - inv_n.shape[0] - inv_m.shape[0],), F32)]).reshape(1, LANES)
    spec = pl.BlockSpec((1, tm, LANES), lambda b, j: (b, j, 0))
    shape = jax.ShapeDtypeStruct((B, S, LANES), F32)
    return pl.pallas_call(
        _rope_table_kernel,
        grid=(B, S // tm),
        in_specs=[pl.BlockSpec((1, tm, 1), lambda b, j: (b, j, 0)), pl.BlockSpec((1, LANES), lambda b, j: (0, 0))],
        out_specs=[spec] * 4,
        out_shape=[shape] * 4,
        compiler_params=_cparams(("parallel", "parallel")),
        name="rope_tables",
    )(pos3, invf)


def _mla_proj_kernel(pd_ref, cos_ref, sin_ref, qn_ref, kvn_ref, wq_ref, wkv_ref, qt_ref, k_ref, vt_ref):
    H = MLA_HEADS
    tm = pd_ref.shape[1]
    cos, sin = cos_ref[0], sin_ref[0]
    cq = (_rms(pd_ref[0, :, 0:256], MLA_Q_RANK) * qn_ref[...]).astype(BF16)
    ckv = (_rms(pd_ref[0, :, 256:384], MLA_KV_RANK) * kvn_ref[...]).astype(BF16)
    kr = _rope_lanes(pd_ref[0, :, 384:512], cos, sin, MLA_ROPE_DIM // 2)
    q_all = _dot(cq, wq_ref[...])
    kv_all = _dot(ckv, wkv_ref[...])
    low = lax.broadcasted_iota(jnp.int32, (tm, LANES), 1) < MLA_V_DIM
    for h in range(H):
        qh = _rope_lanes(q_all[:, h * LANES:(h + 1) * LANES], cos, sin, MLA_ROPE_DIM // 2)
        qt_ref[0, h, 0] = (qh * (MLA_QK_DIM ** -0.5 * LOG2E)).T.astype(BF16)
        k_ref[0, h] = (kv_all[:, h * LANES:(h + 1) * LANES] + kr).astype(BF16)
        vt_ref[0, h, 0] = jnp.where(low, kv_all[:, (H + h) * LANES:(H + h + 1) * LANES], 1.0).T.astype(BF16)


def _mla_proj(pd, cos, sin, qn_p, kvn, wq_p, wkv_p):
    B, S, _ = pd.shape
    H = MLA_HEADS
    tm = MLA_TK
    assert MLA_TQ == tm
    full = lambda shape: pl.BlockSpec(shape, lambda b, j: (0,) * len(shape))
    tspec = pl.BlockSpec((1, H, 1, LANES, tm), lambda b, j: (b, 0, j, 0, 0))
    tshape = jax.ShapeDtypeStruct((B, H, S // tm, LANES, tm), BF16)
    return pl.pallas_call(
        _mla_proj_kernel,
        grid=(B, S // tm),
        in_specs=[pl.BlockSpec((1, tm, WD), lambda b, j: (b, j, 0)),
                  pl.BlockSpec((1, tm, LANES), lambda b, j: (b, j, 0)),
                  pl.BlockSpec((1, tm, LANES), lambda b, j: (b, j, 0)),
                  full((1, 256)), full((1, LANES)), full((256, H * LANES)), full((LANES, 2 * H * LANES))],
        out_specs=[tspec, pl.BlockSpec((1, H, tm, LANES), lambda b, j: (b, 0, j, 0)), tspec],
        out_shape=[tshape, jax.ShapeDtypeStruct((B, H, S, LANES), BF16), tshape],
        compiler_params=_cparams(("parallel", "parallel")),
        name="mla_proj",
    )(pd, cos, sin, qn_p, kvn.reshape(1, LANES), wq_p, wkv_p)


def _softmax_update(s_refs, vts, m_refs, acc_refs, mask):
    ss = [r[...] for r in s_refs]
    if mask is not None:
        ss = [jnp.where(mask, s, NEG_INF) for s in ss]
    m_prevs = [r[...] for r in m_refs]
    m_news = [jnp.maximum(mp, jnp.max(s, axis=0, keepdims=True)) for mp, s in zip(m_prevs, ss)]
    ps = [jnp.exp2(s - mn).astype(BF16) for s, mn in zip(ss, m_news)]
    for vt1, m_ref, acc_ref, p, mp, mn in zip(vts, m_refs, acc_refs, ps, m_prevs, m_news):
        acc_ref[...] = jnp.exp2(mp - mn) * acc_ref[...] + _dot(vt1, p)
        m_ref[...] = mn


def _flash_loop(n_full, scores, values, mask, buf_a, buf_b, m_refs, acc_refs):
    for r in m_refs:
        r[...] = jnp.full(r.shape, NEG_INF, F32)
    for r in acc_refs:
        r[...] = jnp.zeros(r.shape, F32)

    def fill(bufs, j):
        for r, sc in zip(bufs, scores(j)):
            r[...] = sc

    fill(buf_a, 0)

    def body(jj, carry):
        j = 2 * jj
        fill(buf_b, j + 1)
        _softmax_update(buf_a, values(j), m_refs, acc_refs, None)
        fill(buf_a, j + 2)
        _softmax_update(buf_b, values(j + 1), m_refs, acc_refs, None)
        return carry

    lax.fori_loop(0, n_full // 2, body, 0)

    @pl.when(n_full % 2 == 1)
    def _():
        fill(buf_b, n_full)
        _softmax_update(buf_a, values(n_full - 1), m_refs, acc_refs, None)
        _softmax_update(buf_b, values(n_full), m_refs, acc_refs, mask)

    @pl.when(n_full % 2 == 0)
    def _():
        _softmax_update(buf_a, values(n_full), m_refs, acc_refs, mask)


def _normalize(acc):
    return acc / jnp.maximum(acc[V_ONES:V_ONES + 1, :], 1e-30)


def _mla_attn_kernel(qt_ref, k_ref, vt_ref, o_ref, *scratch):
    tq, tk = MLA_TQ, MLA_TK
    H = MLA_HEADS
    m_refs, acc_refs, buf_a, buf_b = (scratch[i * H:(i + 1) * H] for i in range(4))
    s0 = pl.program_id(1) * tq
    n_full = s0 // tk

    def scores(j):
        off = pl.multiple_of(j * tk, tk)
        return [_dot(k_ref[0, h, pl.ds(off, tk), :], qt_ref[0, h, 0]) for h in range(H)]

    def values(j):
        return [vt_ref[0, h, j] for h in range(H)]

    kpos = n_full * tk + lax.broadcasted_iota(jnp.int32, (tk, tq), 0)
    t = s0 + lax.broadcasted_iota(jnp.int32, (tk, tq), 1)
    _flash_loop(n_full, scores, values, kpos <= t, buf_a, buf_b, m_refs, acc_refs)
    o_t = jnp.concatenate([_normalize(acc_refs[h][...])[0:MLA_V_DIM, :] for h in range(H)], axis=0)
    o_ref[0] = o_t.T


def _mla_attn(qt, k, vt):
    B, H, S, _ = k.shape
    tq, tk = MLA_TQ, MLA_TK
    return pl.pallas_call(
        _mla_attn_kernel,
        grid=(B, S // tq),
        in_specs=[pl.BlockSpec((1, H, 1, LANES, tq), lambda b, i: (b, 0, i, 0, 0)),
                  pl.BlockSpec((1, H, S, LANES), lambda b, i: (b, 0, 0, 0)),
                  pl.BlockSpec((1, H, S // tk, LANES, tk), lambda b, i: (b, 0, 0, 0, 0))],
        out_specs=pl.BlockSpec((1, tq, H * MLA_V_DIM), lambda b, i: (b, i, 0)),
        out_shape=jax.ShapeDtypeStruct((B, S, H * MLA_V_DIM), F32),
        scratch_shapes=([pltpu.VMEM((1, tq), F32)] * H + [pltpu.VMEM((LANES, tq), F32)] * H
                        + [pltpu.VMEM((tk, tq), F32)] * (2 * H)),
        compiler_params=_cparams(("parallel", "arbitrary")),
        name="mla_attn",
    )(qt, k, vt)


def _nsa_proj_kernel(pb_ref, cos_ref, sin_ref, qt_ref, ksa_ref, vst_ref, kw_ref, vwt_ref, kcvc_ref):
    tm = pb_ref.shape[1]
    H, TQ = NSA_HEADS, NSA_TQ
    r = _rope_lanes(pb_ref[0, :, 0:512], cos_ref[0], sin_ref[0], NSA_HEAD_DIM // 2)
    nr0 = pb_ref[0, :, 512:640]
    nr1 = pb_ref[0, :, 640:768]
    lane = lax.broadcasted_iota(jnp.int32, (tm, LANES), 1)
    low = lane < NSA_HEAD_DIM
    scale = NSA_HEAD_DIM ** -0.5 * LOG2E
    for h in range(H):
        seg = r[:, (h // 2) * LANES:(h // 2 + 1) * LANES]
        if h % 2:
            seg = pltpu.roll(seg, NSA_HEAD_DIM, 1)
        qh_t = jnp.where(low, seg * scale, 0.0).T.astype(BF16)
        for c in range(tm // TQ):
            qt_ref[0, c, :, h * TQ:(h + 1) * TQ] = qh_t[:, c * TQ:(c + 1) * TQ]
    kseg = r[:, 256:384]
    blk = lax.shift_right_logical(lax.broadcasted_iota(jnp.int32, (tm, LANES), 0) + pl.program_id(1) * tm, 6)
    ksa_ref[0, :, 0:LANES] = jnp.where(low, kseg, 0.0).astype(BF16)
    ksa_ref[0, :, LANES:2 * LANES] = jnp.where(blk == lane, 1.0, 0.0).astype(BF16)
    kw_ref[0] = r[:, 384:512].astype(BF16)
    vst_ref[0, 0] = jnp.where(low, pltpu.roll(nr0, NSA_HEAD_DIM, 1), 1.0).T.astype(BF16)
    vw_t = jnp.where(low, nr1, 1.0).T.astype(BF16)
    for c in range(tm // LANES):
        vwt_ref[0, c] = vw_t[:, c * LANES:(c + 1) * LANES]
    kcvc_ref[0] = jnp.where(low, pltpu.roll(kseg, NSA_HEAD_DIM, 1), pltpu.roll(nr0, NSA_HEAD_DIM, 1))


def _nsa_proj(pb, cos, sin):
    B, S, _ = pb.shape
    H, TQ = NSA_HEADS, NSA_TQ
    tm = NSA_TK
    tok = lambda w: pl.BlockSpec((1, tm, w), lambda b, j: (b, j, 0))
    return pl.pallas_call(
        _nsa_proj_kernel,
        grid=(B, S // tm),
        in_specs=[tok(WB), tok(LANES), tok(LANES)],
        out_specs=[pl.BlockSpec((1, tm // TQ, LANES, H * TQ), lambda b, j: (b, j, 0, 0)),
                   tok(2 * LANES),
                   pl.BlockSpec((1, 1, LANES, tm), lambda b, j: (b, j, 0, 0)),
                   tok(LANES),
                   pl.BlockSpec((1, tm // LANES, LANES, LANES), lambda b, j: (b, j, 0, 0)),
                   tok(LANES)],
        out_shape=[jax.ShapeDtypeStruct((B, S // TQ, LANES, H * TQ), BF16),
                   jax.ShapeDtypeStruct((B, S, 2 * LANES), BF16),
                   jax.ShapeDtypeStruct((B, S // tm, LANES, tm), BF16),
                   jax.ShapeDtypeStruct((B, S, LANES), BF16),
                   jax.ShapeDtypeStruct((B, S // LANES, LANES, LANES), BF16),
                   jax.ShapeDtypeStruct((B, S, LANES), F32)],
        compiler_params=_cparams(("parallel", "parallel")),
        name="nsa_proj",
    )(pb, cos, sin)


def _nsa_cmp_kernel(t_ref, ptop_ref, pbot_ref, wtop_ref, wbot_ref, b1_ref, w2k_ref, w2v_ref, kc_ref, vct_ref, sh_ref):
    n = t_ref.shape[1]
    t = t_ref[0]
    top = _dot((t + ptop_ref[...]).astype(BF16), wtop_ref[...])
    sh_ref[0:n, :] = _dot((t + pbot_ref[...]).astype(BF16), wbot_ref[...])
    sh_ref[n:n + 8, :] = jnp.zeros((8, sh_ref.shape[1]), F32)
    hid = jax.nn.gelu(top + sh_ref[pl.ds(1, n), :] + b1_ref[...])
    kc_ref[0] = _dot(hid[:, 0:CMP_HIDDEN].astype(BF16), w2k_ref[...]).astype(BF16)
    vc = _dot(hid[:, CMP_HIDDEN:2 * CMP_HIDDEN].astype(BF16), w2v_ref[...])
    low = lax.broadcasted_iota(jnp.int32, vc.shape, 1) < NSA_HEAD_DIM
    vct_ref[0] = jnp.where(low, vc, 1.0).T.astype(BF16)


def _nsa_compress(t16, ptop, pbot, wtop, wbot, b1, w2k, w2v):
    B, n, F = t16.shape
    full = lambda shape: pl.BlockSpec(shape, lambda b: (0,) * len(shape))
    return pl.pallas_call(
        _nsa_cmp_kernel,
        grid=(B,),
        in_specs=[pl.BlockSpec((1, n, F), lambda b: (b, 0, 0)), full((1, F)), full((1, F)),
                  full((F, 2 * CMP_HIDDEN)), full((F, 2 * CMP_HIDDEN)), full((1, 2 * CMP_HIDDEN)),
                  full((CMP_HIDDEN, LANES)), full((CMP_HIDDEN, LANES))],
        out_specs=[pl.BlockSpec((1, n, LANES), lambda b: (b, 0, 0)), pl.BlockSpec((1, LANES, n), lambda b: (b, 0, 0))],
        out_shape=[jax.ShapeDtypeStruct((B, n, LANES), BF16), jax.ShapeDtypeStruct((B, LANES, n), BF16)],
        scratch_shapes=[pltpu.VMEM((n + 8, 2 * CMP_HIDDEN), F32)],
        compiler_params=_cparams(("parallel",)),
        name="nsa_compress",
    )(t16, ptop, pbot, wtop, wbot, b1, w2k, w2v)


def _nsa_attn_kernel(qt_ref, g_ref, kc_ref, vct_ref, ovl_ref, ksa_ref, vst_ref, kw_ref, vwt_ref, o_ref, *scratch):
    H, TQ, TK = NSA_HEADS, NSA_TQ, NSA_TK
    M = H * TQ
    NB = MAX_SEL_BLOCKS
    s0 = pl.program_id(1) * TQ
    ncmp = kc_ref.shape[1]
    qt = qt_ref[0, 0]
    t_row = s0 + lax.broadcasted_iota(jnp.int32, (1, M), 1) % TQ

    cend = lax.broadcasted_iota(jnp.int32, (ncmp, M), 0) * CMP_STRIDE + (CMP_LEN - 1)
    s_c = jnp.where(cend <= t_row, _dot(kc_ref[0], qt), NEG_INF)
    m_c = jnp.max(s_c, axis=0, keepdims=True)
    e_c = jnp.exp2(s_c - m_c)
    a_c = _dot(vct_ref[0], e_c.astype(BF16))
    inv_c = jnp.where(m_c > 0.5 * NEG_INF, 1.0 / jnp.maximum(a_c[V_ONES:V_ONES + 1, :], 1e-30), 0.0)
    o_c = a_c * inv_c
    p_c = e_c * inv_c

    WK = WINDOW + TQ
    w0 = pl.multiple_of(jnp.maximum(s0 - WINDOW, 0), TQ)
    dist = t_row - (w0 + lax.broadcasted_iota(jnp.int32, (WK, M), 0))
    in_window = lax.bitcast_convert_type(dist, jnp.uint32) < jnp.uint32(WINDOW)
    s_w = jnp.where(in_window, _dot(kw_ref[0, pl.ds(w0, WK), :], qt), NEG_INF)
    e_w = jnp.exp2(s_w - jnp.max(s_w, axis=0, keepdims=True)).astype(BF16)
    a_w = _dot(vwt_ref[0, w0 // LANES], e_w[0:LANES])
    for c in range(1, WK // LANES):
        a_w = a_w + _dot(vwt_ref[0, w0 // LANES + c], e_w[c * LANES:(c + 1) * LANES])
    o_w = _normalize(a_w)

    psum = p_c[:, 0:TQ] + p_c[:, TQ:2 * TQ] + p_c[:, 2 * TQ:3 * TQ] + p_c[:, 3 * TQ:4 * TQ]
    p_hi = psum.astype(BF16)
    p_lo = (psum - p_hi.astype(F32)).astype(BF16)
    imp = _dot(ovl_ref[...], p_hi) + _dot(ovl_ref[...], p_lo)
    jb = lax.broadcasted_iota(jnp.int32, (NB, TQ), 0)
    tq_l = s0 + lax.broadcasted_iota(jnp.int32, (NB, TQ), 1)
    cur = lax.shift_right_logical(tq_l, 6)
    forced = (jb == 0) | (jb == cur) | (jb == cur - 1)
    score = jnp.where(forced, -3e38, jnp.where(jb * SEL_LEN <= tq_l, imp, NEG_INF))
    jbf = jb.astype(F32)
    sel = jnp.where(forced, 1.0, 0.0)
    for _ in range(SEL_TOPN - 3):
        best = jnp.max(score, axis=0, keepdims=True)
        first = jnp.min(jnp.where(score == best, jbf, float(NB)), axis=0, keepdims=True)
        hit = jbf == first
        sel = jnp.where(hit, 1.0, sel)
        score = jnp.where(hit, -3e38, score)
    bias = ((sel - 1.0) * (-NEG_INF)).astype(BF16)
    q_aug = jnp.concatenate([qt, jnp.concatenate([bias] * H, axis=1)], axis=0)

    NC = NSA_CHAINS
    CW = M // NC
    m_refs, acc_refs, buf_a, buf_b = (scratch[i * NC:(i + 1) * NC] for i in range(4))
    n_full = s0 // TK
    q_cols = [q_aug[:, c * CW:(c + 1) * CW] for c in range(NC)]

    def scores(j):
        k = ksa_ref[0, pl.ds(pl.multiple_of(j * TK, TK), TK), :]
        return [_dot(k, qc) for qc in q_cols]

    def values(j):
        return [vst_ref[0, j]] * NC

    kpos = n_full * TK + lax.broadcasted_iota(jnp.int32, (TK, CW), 0)
    _flash_loop(n_full, scores, values, kpos <= t_row[:, 0:CW], buf_a, buf_b, m_refs, acc_refs)
    o_s = jnp.concatenate([_normalize(acc_refs[c][...]) for c in range(NC)], axis=1)

    g = jax.nn.sigmoid(g_ref[0]).T
    outs = []
    for h in range(H):
        cols = slice(h * TQ, (h + 1) * TQ)
        r0 = NSA_HEAD_DIM + 3 * h
        o_h = (g[r0:r0 + 1, :] * o_c[0:NSA_HEAD_DIM, cols] + g[r0 + 1:r0 + 2, :] * o_s[0:NSA_HEAD_DIM, cols]
               + g[r0 + 2:r0 + 3, :] * o_w[0:NSA_HEAD_DIM, cols])
        outs.append(o_h)
    o_ref[0] = jnp.concatenate(outs, axis=0).T


def _nsa_attn(qt, pb, kc, vct, ovl, ksa, vst, kw, vwt):
    B, S, _ = ksa.shape
    H, TQ, TK = NSA_HEADS, NSA_TQ, NSA_TK
    ncmp = kc.shape[1]
    per_b = lambda *shape: pl.BlockSpec((1,) + shape, lambda b, i: (b,) + (0,) * len(shape))
    return pl.pallas_call(
        _nsa_attn_kernel,
        grid=(B, S // TQ),
        in_specs=[pl.BlockSpec((1, 1, LANES, H * TQ), lambda b, i: (b, i, 0, 0)),
                  pl.BlockSpec((1, TQ, LANES), lambda b, i: (b, i, WB // LANES - 1)),
                  per_b(ncmp, LANES), per_b(LANES, ncmp),
                  pl.BlockSpec((MAX_SEL_BLOCKS, ncmp), lambda b, i: (0, 0)),
                  per_b(S, 2 * LANES), per_b(S // TK, LANES, TK), per_b(S, LANES), per_b(S // LANES, LANES, LANES)],
        out_specs=pl.BlockSpec((1, TQ, H * NSA_HEAD_DIM), lambda b, i: (b, i, 0)),
        out_shape=jax.ShapeDtypeStruct((B, S, H * NSA_HEAD_DIM), F32),
        scratch_shapes=([pltpu.VMEM((1, H * TQ // NSA_CHAINS), F32)] * NSA_CHAINS
                        + [pltpu.VMEM((LANES, H * TQ // NSA_CHAINS), F32)] * NSA_CHAINS
                        + [pltpu.VMEM((TK, H * TQ // NSA_CHAINS), F32)] * (2 * NSA_CHAINS)),
        compiler_params=_cparams(("parallel", "arbitrary")),
        name="nsa_attn",
    )(qt, pb, kc, vct, ovl, ksa, vst, kw, vwt)


def _nsa_overlap(ncmp):
    n = np.arange(ncmp)[None, :] * CMP_STRIDE
    s = np.arange(MAX_SEL_BLOCKS)[:, None] * SEL_LEN
    real = np.arange(ncmp)[None, :] < ncmp - 1
    ovl = (n < s + SEL_LEN) & (n + CMP_LEN > s) & real
    return jnp.asarray(ovl.astype(np.float32), dtype=BF16)


def _mix_ffn_kernel(ya_ref, yb_ref, yc_ref, yd_ref, x_ref, gn_ref, wo_ref, g_ref, wgu_ref, wd_ref, fn_ref, o_ref, *,
                    final_norm):
    x = x_ref[...]
    for g, y_ref in enumerate((ya_ref, yb_ref, yc_ref, yd_ref)):
        sl = slice(g * GROUP_WIDTH, (g + 1) * GROUP_WIDTH)
        y = (_rms(y_ref[...], GROUP_WIDTH) * gn_ref[:, sl]).astype(BF16)
        x = x + _dot(y, wo_ref[sl, :])
    o_ref[...] = x
    x = o_ref[...]
    h = (_rms(x, D_MODEL) * g_ref[...]).astype(BF16)
    acc = x
    for c in range(D_FF // FF_CHUNK):
        lo = c * FF_CHUNK
        gate = _dot(h, wgu_ref[:, lo:lo + FF_CHUNK])
        up = _dot(h, wgu_ref[:, D_FF + lo:D_FF + lo + FF_CHUNK])
        acc = acc + _dot((jax.nn.silu(gate) * up).astype(BF16), wd_ref[lo:lo + FF_CHUNK, :])
    if final_norm:
        acc = _rms(acc, D_MODEL) * fn_ref[...]
    o_ref[...] = acc


def _mix_ffn(ys, x2, gn, w_out_b, gain, wgu_b, wd_b, fnorm, layer, final_norm, tm=512):
    T = x2.shape[0]
    W = GROUP_WIDTH
    row = pl.BlockSpec((1, D_MODEL), lambda i: (0, 0))
    resident = lambda shape: pl.BlockSpec((None,) + shape, lambda i: (layer, 0, 0))
    return pl.pallas_call(
        functools.partial(_mix_ffn_kernel, final_norm=final_norm),
        grid=(T // tm,),
        in_specs=[pl.BlockSpec((tm, W), lambda i: (i, 0))] * 4
        + [pl.BlockSpec((tm, D_MODEL), lambda i: (i, 0)), row, resident((D_MODEL, D_MODEL)), row,
           resident((D_MODEL, 2 * D_FF)), resident((D_FF, D_MODEL)), row],
        out_specs=pl.BlockSpec((tm, D_MODEL), lambda i: (i, 0)),
        out_shape=jax.ShapeDtypeStruct((T, D_MODEL), F32),
        compiler_params=_cparams(("parallel",)),
        name="mix_ffn",
    )(*ys, x2, gn.reshape(1, D_MODEL), w_out_b, gain.reshape(1, D_MODEL), wgu_b, wd_b, fnorm.reshape(1, D_MODEL))


def _gather_cols(w, layout):
    parts = []
    for name, width in layout:
        if name is None:
            parts.append(jnp.zeros((w.shape[0], width), w.dtype))
        else:
            o, n = _OFF[name]
            assert n == width
            parts.append(w[:, o:o + n])
    return jnp.concatenate(parts, axis=1)


def _block_diag(w):
    h, d, _ = w.shape
    out = jnp.zeros((h * d, h * d), w.dtype)
    for i in range(h):
        out = out.at[i * d:(i + 1) * d, i * d:(i + 1) * d].set(w[i])
    return out


def _pad_to(a, shape):
    return jnp.pad(a, [(0, s - d) for s, d in zip(shape, a.shape)])


def _mla_weights(w_uq, w_ukv):
    H = MLA_HEADS
    wq = w_uq.reshape(MLA_Q_RANK, H, MLA_QK_DIM)
    wq = _pad_to(wq, (256, H, LANES)).reshape(256, H * LANES)
    wkv = w_ukv.reshape(MLA_KV_RANK, H, MLA_NOPE_DIM + MLA_V_DIM)
    wk = _pad_to(wkv[:, :, :MLA_NOPE_DIM], (MLA_KV_RANK, H, LANES)).reshape(MLA_KV_RANK, H * LANES)
    wv = _pad_to(wkv[:, :, MLA_NOPE_DIM:], (MLA_KV_RANK, H, LANES)).reshape(MLA_KV_RANK, H * LANES)
    return wq.astype(BF16), jnp.concatenate([wk, wv], axis=1).astype(BF16)


def _cmp_weights(cmp_pos, cmp_w1, cmp_b1, cmp_w2):
    half = CMP_LEN // 2
    Dh = NSA_HEAD_DIM
    w1 = cmp_w1.reshape(2, CMP_LEN, Dh, CMP_HIDDEN)

    def rows(part):
        wk = w1[0, part * half:(part + 1) * half]
        wv = w1[1, part * half:(part + 1) * half]
        z = jnp.zeros_like(wk)
        top = jnp.concatenate([wk, z], axis=-1)
        bot = jnp.concatenate([z, wv], axis=-1)
        return jnp.concatenate([top, bot], axis=1).reshape(half * 2 * Dh, 2 * CMP_HIDDEN)

    def pos_row(part):
        p = jnp.concatenate([cmp_pos[0, part * half:(part + 1) * half], cmp_pos[1, part * half:(part + 1) * half]],
                            axis=-1)
        return p.reshape(1, half * 2 * Dh)

    b1 = jnp.concatenate([cmp_b1[0], cmp_b1[1]]).reshape(1, 2 * CMP_HIDDEN)
    w2k = _pad_to(cmp_w2[0], (CMP_HIDDEN, LANES)).astype(BF16)
    w2v = _pad_to(cmp_w2[1], (CMP_HIDDEN, LANES)).astype(BF16)
    return pos_row(0), pos_row(1), rows(0).astype(BF16), rows(1).astype(BF16), b1, w2k, w2v


def kernel(x, positions, norm_mix, w_in, conv_w, conv_b, lru_wa, lru_ba, lru_wx, lru_bx, lru_lambda, cmp_pos, cmp_w1, cmp_b1, cmp_w2, gla_wg2, gla_bg2, gla_norm, mla_q_norm, mla_kv_norm, mla_w_uq, mla_w_ukv, group_norm, w_out, norm_ffn, w_gate_up, w_down, final_norm):
    B, S, D = x.shape
    assert D == D_MODEL and S % (CMP_STRIDE * 8) == 0 and S // SEL_LEN <= MAX_SEL_BLOCKS and S >= WINDOW + NSA_TQ
    depth = w_in.shape[0]
    T = B * S
    pos3 = positions.reshape(B, S, 1)
    ncmp = S // CMP_STRIDE
    ovl = _nsa_overlap(ncmp)
    cos_n, sin_n, cos_m, sin_m = _rope_tables(pos3)
    x2 = x.reshape(T, D)
    w_out_b, wgu_b, wd_b = w_out.astype(BF16), w_gate_up.astype(BF16), w_down.astype(BF16)
    for l in range(depth):
        w_p = jnp.concatenate([_gather_cols(w_in[l], lay) for lay in (_LAYOUT_A, _LAYOUT_B, _LAYOUT_C, _LAYOUT_D)],
                              axis=1).astype(BF16)
        pa, pb, pc, pd = _inproj(x2, norm_mix[l], w_p)
        pa, pb, pc, pd = (p.reshape(B, S, -1) for p in (pa, pb, pc, pd))

        wg = jnp.concatenate([_block_diag(lru_wa[l]), _block_diag(lru_wx[l])], axis=1).astype(BF16)
        bg = jnp.concatenate([lru_ba[l], lru_bx[l]]).reshape(1, 2 * GROUP_WIDTH)
        y_a = _rglru(pa, conv_w[l], conv_b[l], wg, bg, lru_lambda[l])

        qt_n, ksa, vst, kw, vwt, kcvc = _nsa_proj(pb, cos_n, sin_n)
        t16 = kcvc.reshape(B, ncmp, CMP_STRIDE * 2 * NSA_HEAD_DIM)
        kc, vct = _nsa_compress(t16, *_cmp_weights(cmp_pos[l], cmp_w1[l], cmp_b1[l], cmp_w2[l]))
        y_b = _nsa_attn(qt_n, pb, kc, vct, ovl, ksa, vst, kw, vwt)

        wg2_p = _pad_to(gla_wg2[l], (LANES, GLA_HEADS * GLA_DK))
        y_c = _gla(pc, wg2_p, gla_bg2[l], jnp.tile(gla_norm[l], GLA_HEADS))

        wq_p, wkv_p = _mla_weights(mla_w_uq[l], mla_w_ukv[l])
        qt_m, k_m, vt_m = _mla_proj(pd, cos_m, sin_m, _pad_to(mla_q_norm[l], (256,)).reshape(1, 256), mla_kv_norm[l],
                                    wq_p, wkv_p)
        y_d = _mla_attn(qt_m, k_m, vt_m)

        ys = [y.reshape(T, GROUP_WIDTH) for y in (y_a, y_b, y_c, y_d)]
        x2 = _mix_ffn(ys, x2, group_norm[l], w_out_b, norm_ffn[l], wgu_b, wd_b, final_norm, l,
                      final_norm=(l == depth - 1))
    return x2.reshape(B, S, D)
```

```python
import functools

import numpy as np
import jax
import jax.numpy as jnp
from jax import lax
from jax.experimental import pallas as pl
from jax.experimental.pallas import tpu as pltpu

F32 = jnp.float32
BF16 = jnp.bfloat16

D_MODEL = 1024
GROUP_WIDTH = 256
ROPE_THETA = 10000.0
NORM_EPS = 1e-6
NEG_INF = -1e30
FORCE_SCORE = 1e9

LRU_C = 8.0
CONV_WIDTH = 4

NSA_HEADS = 4
NSA_HEAD_DIM = 64
CMP_LEN = 32
CMP_STRIDE = 16
CMP_HIDDEN = 256
SEL_LEN = 64
SEL_TOPN = 16
WINDOW = 512
NSA_TQ = 256
NSA_TK = 512
NSA_CHAINS = 2
MAX_SEL_BLOCKS = 128

GLA_HEADS = 4
GLA_DV = 64
GLA_DK = 32
GLA_GATE_RANK = 16
GLA_TAU = 16.0
GLA_TILE = 256
GLA_SUB = 16

MLA_HEADS = 4
MLA_V_DIM = 64
MLA_NOPE_DIM = 64
MLA_ROPE_DIM = 32
MLA_QK_DIM = 96
MLA_Q_RANK = 192
MLA_KV_RANK = 128
MLA_TQ = 512
MLA_TK = 512

D_FF = 2816
FF_CHUNK = 256

LANES = 128
V_ONES = 64
LOG2E = 1.4426950408889634
VMEM_LIMIT = 56 * 1024 * 1024

_OFF = {}
_o = 0
for _n, _w in (("a_x", 256), ("a_gate", 256), ("b_q", 256), ("k_c", 64), ("v_c", 64), ("k_s", 64), ("v_s", 64),
               ("k_w", 64), ("v_w", 64), ("b_gate", 12), ("c_q", 128), ("c_k", 128), ("c_v", 256), ("c_glr", 16),
               ("c_og", 256), ("d_cq", 192), ("d_ckv", 128), ("d_kr", 32)):
    _OFF[_n] = (_o, _w)
    _o += _w
IN_COLS = _o
_LAYOUT_A = (("a_x", 256), ("a_gate", 256))
_LAYOUT_B = (("b_q", 256), ("k_s", 64), ("k_c", 64), ("k_w", 64), (None, 64),
             ("v_c", 64), ("v_s", 64), ("v_w", 64), ("b_gate", 12), (None, 52))
_LAYOUT_C = (("c_q", 128), ("c_k", 128), ("c_v", 256), ("c_og", 256), ("c_glr", 16), (None, 112))
_LAYOUT_D = (("d_cq", 192), (None, 64), ("d_ckv", 128), (None, 64), ("d_kr", 32), (None, 32))
WA, WB, WC, WD = 512, 768, 896, 512


def _cparams(sem):
    return pltpu.CompilerParams(dimension_semantics=sem, vmem_limit_bytes=VMEM_LIMIT)


def _dot(a, b):
    return jnp.dot(a, b, preferred_element_type=F32)


def _dot_nt(a, b):
    return lax.dot_general(a, b, (((1,), (1,)), ((), ())), preferred_element_type=F32)


def _dot_tn(a, b):
    return lax.dot_general(a, b, (((0,), (0,)), ((), ())), preferred_element_type=F32)


def _split2(x):
    hi = x.astype(BF16)
    return hi, (x - hi.astype(F32)).astype(BF16)


def _dot_split(a, b, lhs_split=False):
    if lhs_split:
        hi, lo = _split2(a)
        return _dot(hi, b) + _dot(lo, b)
    hi, lo = _split2(b)
    return _dot(a, hi) + _dot(a, lo)


def _dot_x3(a, b):
    a_hi, a_lo = _split2(a)
    b_hi, b_lo = _split2(b)
    return _dot(a_hi, b_hi) + (_dot(a_hi, b_lo) + _dot(a_lo, b_hi))


def _rms(x, width):
    return x * lax.rsqrt(jnp.sum(x * x, axis=-1, keepdims=True) / width + NORM_EPS)


def _inproj_kernel(x_ref, g_ref, w_ref, oa_ref, ob_ref, oc_ref, od_ref):
    h = (_rms(x_ref[...], D_MODEL) * g_ref[...]).astype(BF16)
    off = 0
    for o_ref in (oa_ref, ob_ref, oc_ref, od_ref):
        w = o_ref.shape[1]
        o_ref[...] = _dot(h, w_ref[:, off:off + w])
        off += w


def _inproj(x2, gain, w_p, layer, tm=512):
    T = x2.shape[0]
    ntot = w_p.shape[2]
    return pl.pallas_call(
        _inproj_kernel,
        grid=(T // tm,),
        in_specs=[pl.BlockSpec((tm, D_MODEL), lambda i: (i, 0)),
                  pl.BlockSpec((1, D_MODEL), lambda i: (0, 0)),
                  pl.BlockSpec((None, D_MODEL, ntot), lambda i: (layer, 0, 0))],
        out_specs=[pl.BlockSpec((tm, w), lambda i: (i, 0)) for w in (WA, WB, WC, WD)],
        out_shape=[jax.ShapeDtypeStruct((T, w), F32) for w in (WA, WB, WC, WD)],
        compiler_params=_cparams(("parallel",)),
        name="inproj",
    )(x2, gain.reshape(1, D_MODEL), w_p)


def _rglru_kernel(pa_ref, cw_ref, cb_ref, wg_ref, bg_ref, lam_ref, o_ref, xbuf, a_s, u_s, h_s, hlast):
    ts = pa_ref.shape[1]
    W = GROUP_WIDTH

    @pl.when(pl.program_id(1) == 0)
    def _():
        xbuf[0:8, :] = jnp.zeros((8, W), F32)
        hlast[...] = jnp.zeros_like(hlast)

    xbuf[8:8 + ts, :] = pa_ref[0, :, 0:W]
    xc = cb_ref[...]
    for k in range(CONV_WIDTH):
        lo = 8 - (CONV_WIDTH - 1) + k
        xc = xc + cw_ref[k:k + 1, :] * xbuf[lo:lo + ts, :]
    xbuf[0:8, :] = xbuf[ts:ts + 8, :]
    gates = _dot(xc.astype(BF16), wg_ref[...]) + bg_ref[...]
    r = jax.nn.sigmoid(gates[:, 0:W])
    i = jax.nn.sigmoid(gates[:, W:2 * W])
    log_a = (-LRU_C) * r * jax.nn.softplus(-lam_ref[...])
    a = jnp.exp(log_a)
    u = jnp.sqrt(-jnp.tanh(log_a) * (a * a + 1.0)) * (i * xc)
    a_s[...] = a
    u_s[...] = u

    def body(t, h):
        h = a_s[pl.ds(t, 1), :] * h + u_s[pl.ds(t, 1), :]
        h_s[pl.ds(t, 1), :] = h
        return h

    hlast[...] = lax.fori_loop(0, ts, body, hlast[...], unroll=8)
    o_ref[0] = h_s[...] * jax.nn.gelu(pa_ref[0, :, W:2 * W])


def _rglru(pa, conv_w, conv_b, wg, bg, lam, ts=512):
    B, S, _ = pa.shape
    W = GROUP_WIDTH
    full = lambda shape: pl.BlockSpec(shape, lambda b, j: (0,) * len(shape))
    return pl.pallas_call(
        _rglru_kernel,
        grid=(B, S // ts),
        in_specs=[pl.BlockSpec((1, ts, WA), lambda b, j: (b, j, 0)),
                  full((CONV_WIDTH, W)), full((1, W)), full((W, 2 * W)), full((1, 2 * W)), full((1, W))],
        out_specs=pl.BlockSpec((1, ts, W), lambda b, j: (b, j, 0)),
        out_shape=jax.ShapeDtypeStruct((B, S, W), F32),
        scratch_shapes=[pltpu.VMEM((ts + 8, W), F32), pltpu.VMEM((ts, W), F32), pltpu.VMEM((ts, W), F32),
                        pltpu.VMEM((ts, W), F32), pltpu.VMEM((1, W), F32)],
        compiler_params=_cparams(("parallel", "arbitrary")),
        name="rglru",
    )(pa, conv_w, conv_b.reshape(1, W), wg, bg, lam.reshape(1, W))


def _gla_kernel(pc_ref, wg2_ref, bg2_ref, gn_ref, lcs_ref, hsum_ref, gmean_ref, smask_ref, o_ref, st_ref):
    TT, SUB = GLA_TILE, GLA_SUB
    NB = TT // SUB
    KW = GLA_HEADS * GLA_DK
    VW = GLA_HEADS * GLA_DV

    @pl.when(pl.program_id(1) == 0)
    def _():
        st_ref[...] = jnp.zeros_like(st_ref)

    q = pc_ref[0, :, 0:KW] * (GLA_DK ** -0.5)
    k = pc_ref[0, :, KW:2 * KW]
    v = pc_ref[0, :, 2 * KW:2 * KW + VW]
    og = pc_ref[0, :, 2 * KW + VW:2 * KW + 2 * VW]
    glr = pc_ref[0, :, 2 * KW + 2 * VW:2 * KW + 2 * VW + LANES]
    log_a = jax.nn.log_sigmoid(_dot_x3(glr, wg2_ref[...]) + bg2_ref[...]) * (1.0 / GLA_TAU)
    cums = _dot_split(lcs_ref[...], log_a)
    b = cums[0:TT]
    bl = cums[TT:2 * TT]

    q3 = q.reshape(NB, SUB, KW)
    k3 = k.reshape(NB, SUB, KW)
    b3 = b.reshape(NB, SUB, KW)
    v3 = v.reshape(NB, SUB, VW)
    row = lax.broadcasted_iota(jnp.int32, (NB, SUB, KW), 1)
    terms = []
    for j in range(SUB):
        kj = jnp.broadcast_to(k3[:, j:j + 1, :], (NB, SUB, KW))
        bj = jnp.broadcast_to(b3[:, j:j + 1, :], (NB, SUB, KW))
        e = q3 * kj * jnp.exp(jnp.where(row >= j, b3 - bj, NEG_INF))
        terms.append(e.reshape(TT, KW))
    e_all = jnp.concatenate(terms, axis=0)
    attn = _dot(e_all.astype(BF16), hsum_ref[...])
    o = jnp.zeros((NB, SUB, VW), F32)
    for j in range(SUB):
        vj = jnp.broadcast_to(v3[:, j:j + 1, :], (NB, SUB, VW))
        o = o + attn[j * TT:(j + 1) * TT, :].reshape(NB, SUB, VW) * vj
    o = o.reshape(TT, VW)

    qd = (q * jnp.exp(b)).astype(BF16)
    kd = (k * jnp.exp(bl - b)).astype(BF16)
    dec = jnp.exp(bl)
    vb = v.astype(BF16)
    blocks = [slice(n * SUB, (n + 1) * SUB) for n in range(NB)]
    upds = [_dot_tn(vb[rows], kd[rows]) * smask_ref[...] for rows in blocks]
    st = st_ref[...]
    sts = []
    for n in range(NB):
        sts.append(st.astype(BF16))
        st = st * dec[n * SUB:n * SUB + 1, :] + upds[n]
    st_ref[...] = st
    o = o + jnp.concatenate([_dot_nt(qd[rows], s_n) for rows, s_n in zip(blocks, sts)], axis=0)

    ms = _dot_split(o * o, gmean_ref[...], lhs_split=True)
    o_ref[0] = o * lax.rsqrt(ms + NORM_EPS) * gn_ref[...] * jax.nn.silu(og)


def _gla_consts():
    TT, SUB = GLA_TILE, GLA_SUB
    KW, VW = GLA_HEADS * GLA_DK, GLA_HEADS * GLA_DV
    i = np.arange(TT)
    same = (i[:, None] // SUB) == (i[None, :] // SUB)
    lcum = (same & (i[None, :] <= i[:, None])).astype(np.float32)
    lsum = same.astype(np.float32)
    hk = np.arange(KW) // GLA_DK
    hv = np.arange(VW) // GLA_DV
    hsum = (hk[:, None] == hv[None, :]).astype(np.float32)
    gmean = (hv[:, None] == hv[None, :]).astype(np.float32) / GLA_DV
    smask = (hv[:, None] == hk[None, :]).astype(np.float32)
    return (jnp.asarray(np.concatenate([lcum, lsum], axis=0), dtype=BF16), jnp.asarray(hsum, dtype=BF16),
            jnp.asarray(gmean, dtype=BF16), jnp.asarray(smask))


def _gla(pc, wg2_p, bg2, gn_t):
    B, S, _ = pc.shape
    TT = GLA_TILE
    KW, VW = GLA_HEADS * GLA_DK, GLA_HEADS * GLA_DV
    lcs, hsum, gmean, smask = _gla_consts()
    full = lambda shape: pl.BlockSpec(shape, lambda b, j: (0,) * len(shape))
    return pl.pallas_call(
        _gla_kernel,
        grid=(B, S // TT),
        in_specs=[pl.BlockSpec((1, TT, WC), lambda b, j: (b, j, 0)),
                  full((LANES, KW)), full((1, KW)), full((1, VW)), full((2 * TT, TT)),
                  full((KW, VW)), full((VW, VW)), full((VW, KW))],
        out_specs=pl.BlockSpec((1, TT, VW), lambda b, j: (b, j, 0)),
        out_shape=jax.ShapeDtypeStruct((B, S, VW), F32),
        scratch_shapes=[pltpu.VMEM((VW, KW), F32)],
        compiler_params=_cparams(("parallel", "arbitrary")),
        name="gla",
    )(pc, wg2_p, bg2.reshape(1, KW), gn_t.reshape(1, VW), lcs, hsum, gmean, smask)


def _rope_lanes(x, cos, sin_signed, half):
    lane = lax.broadcasted_iota(jnp.int32, cos.shape, 1)
    lo = (lane % (2 * half)) < half
    outs = []
    for c in range(x.shape[1] // LANES):
        xs = x[:, c * LANES:(c + 1) * LANES]
        rot = jnp.where(lo, pltpu.roll(xs, LANES - half, 1), pltpu.roll(xs, half, 1))
        outs.append(xs * cos + rot * sin_signed)
    return outs[0] if len(outs) == 1 else jnp.concatenate(outs, axis=1)


def _rope_table_kernel(pos_ref, invf_ref, cn_ref, sn_ref, cm_ref, sm_ref):
    ang = pos_ref[0].astype(F32) * invf_ref[...]
    lane = lax.broadcasted_iota(jnp.int32, ang.shape, 1)
    hn, hm = NSA_HEAD_DIM // 2, MLA_ROPE_DIM // 2
    for src, n_ref, m_ref, is_sin in ((jnp.cos(ang), cn_ref, cm_ref, False), (jnp.sin(ang), sn_ref, sm_ref, True)):
        t = jnp.where(lane < hn, src, pltpu.roll(src, hn, 1))
        t = jnp.where(lane < 2 * hn, t, pltpu.roll(t, 2 * hn, 1))
        m = jnp.where(lane < MLA_NOPE_DIM + hm, pltpu.roll(src, MLA_NOPE_DIM - hn, 1),
                      pltpu.roll(src, MLA_NOPE_DIM - hn + hm, 1))
        rope_m = (lane >= MLA_NOPE_DIM) & (lane < MLA_QK_DIM)
        if is_sin:
            n_ref[0] = jnp.where(lane % (2 * hn) < hn, -t, t)
            m_ref[0] = jnp.where(rope_m, jnp.where(lane % (2 * hm) < hm, -m, m), 0.0)
        else:
            n_ref[0] = t
            m_ref[0] = jnp.where(rope_m, m, 1.0)


def _rope_tables(pos3, tm=1024):
    B, S, _ = pos3.shape
    inv_n = ROPE_THETA ** (-jnp.arange(0, NSA_HEAD_DIM, 2, dtype=F32) / NSA_HEAD_DIM)
    inv_m = ROPE_THETA ** (-jnp.arange(0, MLA_ROPE_DIM, 2, dtype=F32) / MLA_ROPE_DIM)
    invf = jnp.concatenate([inv_n, inv_m, jnp.zeros((LANES - inv_n.shape[0] - inv_m.shape[0],), F32)]).reshape(1, LANES)
    spec = pl.BlockSpec((1, tm, LANES), lambda b, j: (b, j, 0))
    shape = jax.ShapeDtypeStruct((B, S, LANES), F32)
    return pl.pallas_call(
        _rope_table_kernel,
        grid=(B, S // tm),
        in_specs=[pl.BlockSpec((1, tm, 1), lambda b, j: (b, j, 0)), pl.BlockSpec((1, LANES), lambda b, j: (0, 0))],
        out_specs=[spec] * 4,
        out_shape=[shape] * 4,
        compiler_params=_cparams(("parallel", "parallel")),
        name="rope_tables",
    )(pos3, invf)


def _mla_proj_kernel(pd_ref, cos_ref, sin_ref, qn_ref, kvn_ref, wq_ref, wkv_ref, qt_ref, k_ref, vt_ref):
    H = MLA_HEADS
    tm = pd_ref.shape[1]
    cos, sin = cos_ref[0], sin_ref[0]
    cq = (_rms(pd_ref[0, :, 0:256], MLA_Q_RANK) * qn_ref[...]).astype(BF16)
    ckv = (_rms(pd_ref[0, :, 256:384], MLA_KV_RANK) * kvn_ref[...]).astype(BF16)
    kr = _rope_lanes(pd_ref[0, :, 384:512], cos, sin, MLA_ROPE_DIM // 2)
    q_all = _dot(cq, wq_ref[...])
    kv_all = _dot(ckv, wkv_ref[...])
    low = lax.broadcasted_iota(jnp.int32, (tm, LANES), 1) < MLA_V_DIM
    for h in range(H):
        qh = _rope_lanes(q_all[:, h * LANES:(h + 1) * LANES], cos, sin, MLA_ROPE_DIM // 2)
        qt_ref[0, h, 0] = (qh * (MLA_QK_DIM ** -0.5 * LOG2E)).T.astype(BF16)
        k_ref[0, h] = (kv_all[:, h * LANES:(h + 1) * LANES] + kr).astype(BF16)
        vt_ref[0, h, 0] = jnp.where(low, kv_all[:, (H + h) * LANES:(H + h + 1) * LANES], 1.0).T.astype(BF16)


def _mla_proj(pd, cos, sin, qn_p, kvn, wq_p, wkv_p):
    B, S, _ = pd.shape
    H = MLA_HEADS
    tm = MLA_TK
    assert MLA_TQ == tm
    full = lambda shape: pl.BlockSpec(shape, lambda b, j: (0,) * len(shape))
    tspec = pl.BlockSpec((1, H, 1, LANES, tm), lambda b, j: (b, 0, j, 0, 0))
    tshape = jax.ShapeDtypeStruct((B, H, S // tm, LANES, tm), BF16)
    return pl.pallas_call(
        _mla_proj_kernel,
        grid=(B, S // tm),
        in_specs=[pl.BlockSpec((1, tm, WD), lambda b, j: (b, j, 0)),
                  pl.BlockSpec((1, tm, LANES), lambda b, j: (b, j, 0)),
                  pl.BlockSpec((1, tm, LANES), lambda b, j: (b, j, 0)),
                  full((1, 256)), full((1, LANES)), full((256, H * LANES)), full((LANES, 2 * H * LANES))],
        out_specs=[tspec, pl.BlockSpec((1, H, tm, LANES), lambda b, j: (b, 0, j, 0)), tspec],
        out_shape=[tshape, jax.ShapeDtypeStruct((B, H, S, LANES), BF16), tshape],
        compiler_params=_cparams(("parallel", "parallel")),
        name="mla_proj",
    )(pd, cos, sin, qn_p, kvn.reshape(1, LANES), wq_p, wkv_p)


def _softmax_update(s_refs, vts, m_refs, acc_refs, mask):
    ss = [r[...] for r in s_refs]
    if mask is not None:
        ss = [jnp.where(mask, s, NEG_INF) for s in ss]
    m_prevs = [r[...] for r in m_refs]
    m_news = [jnp.maximum(mp, jnp.max(s, axis=0, keepdims=True)) for mp, s in zip(m_prevs, ss)]
    ps = [jnp.exp2(s - mn).astype(BF16) for s, mn in zip(ss, m_news)]
    for vt1, m_ref, acc_ref, p, mp, mn in zip(vts, m_refs, acc_refs, ps, m_prevs, m_news):
        acc_ref[...] = jnp.exp2(mp - mn) * acc_ref[...] + _dot(vt1, p)
        m_ref[...] = mn


def _flash_loop(n_full, scores, values, mask, buf_a, buf_b, m_refs, acc_refs):
    for r in m_refs:
        r[...] = jnp.full(r.shape, NEG_INF, F32)
    for r in acc_refs:
        r[...] = jnp.zeros(r.shape, F32)

    def fill(bufs, j):
        for r, sc in zip(bufs, scores(j)):
            r[...] = sc

    fill(buf_a, 0)

    def body(jj, carry):
        j = 2 * jj
        fill(buf_b, j + 1)
        _softmax_update(buf_a, values(j), m_refs, acc_refs, None)
        fill(buf_a, j + 2)
        _softmax_update(buf_b, values(j + 1), m_refs, acc_refs, None)
        return carry

    lax.fori_loop(0, n_full // 2, body, 0)

    @pl.when(n_full % 2 == 1)
    def _():
        fill(buf_b, n_full)
        _softmax_update(buf_a, values(n_full - 1), m_refs, acc_refs, None)
        _softmax_update(buf_b, values(n_full), m_refs, acc_refs, mask)

    @pl.when(n_full % 2 == 0)
    def _():
        _softmax_update(buf_a, values(n_full), m_refs, acc_refs, mask)


def _normalize(acc):
    return acc / jnp.maximum(acc[V_ONES:V_ONES + 1, :], 1e-30)


def _mla_attn_kernel(qt_ref, k_ref, vt_ref, o_ref, *scratch):
    tq, tk = MLA_TQ, MLA_TK
    H = MLA_HEADS
    m_refs, acc_refs, buf_a, buf_b = (scratch[i * H:(i + 1) * H] for i in range(4))
    s0 = pl.program_id(1) * tq
    n_full = s0 // tk

    def scores(j):
        off = pl.multiple_of(j * tk, tk)
        return [_dot(k_ref[0, h, pl.ds(off, tk), :], qt_ref[0, h, 0]) for h in range(H)]

    def values(j):
        return [vt_ref[0, h, j] for h in range(H)]

    kpos = n_full * tk + lax.broadcasted_iota(jnp.int32, (tk, tq), 0)
    t = s0 + lax.broadcasted_iota(jnp.int32, (tk, tq), 1)
    _flash_loop(n_full, scores, values, kpos <= t, buf_a, buf_b, m_refs, acc_refs)
    o_t = jnp.concatenate([_normalize(acc_refs[h][...])[0:MLA_V_DIM, :] for h in range(H)], axis=0)
    o_ref[0] = o_t.T


def _mla_attn(qt, k, vt):
    B, H, S, _ = k.shape
    tq, tk = MLA_TQ, MLA_TK
    return pl.pallas_call(
        _mla_attn_kernel,
        grid=(B, S // tq),
        in_specs=[pl.BlockSpec((1, H, 1, LANES, tq), lambda b, i: (b, 0, i, 0, 0)),
                  pl.BlockSpec((1, H, S, LANES), lambda b, i: (b, 0, 0, 0)),
                  pl.BlockSpec((1, H, S // tk, LANES, tk), lambda b, i: (b, 0, 0, 0, 0))],
        out_specs=pl.BlockSpec((1, tq, H * MLA_V_DIM), lambda b, i: (b, i, 0)),
        out_shape=jax.ShapeDtypeStruct((B, S, H * MLA_V_DIM), F32),
        scratch_shapes=([pltpu.VMEM((1, tq), F32)] * H + [pltpu.VMEM((LANES, tq), F32)] * H
                        + [pltpu.VMEM((tk, tq), F32)] * (2 * H)),
        compiler_params=_cparams(("parallel", "arbitrary")),
        name="mla_attn",
    )(qt, k, vt)


def _nsa_proj_kernel(pb_ref, cos_ref, sin_ref, qt_ref, ksa_ref, vst_ref, kw_ref, vwt_ref, kcvc_ref):
    tm = pb_ref.shape[1]
    H, TQ = NSA_HEADS, NSA_TQ
    r = _rope_lanes(pb_ref[0, :, 0:512], cos_ref[0], sin_ref[0], NSA_HEAD_DIM // 2)
    nr0 = pb_ref[0, :, 512:640]
    nr1 = pb_ref[0, :, 640:768]
    lane = lax.broadcasted_iota(jnp.int32, (tm, LANES), 1)
    low = lane < NSA_HEAD_DIM
    scale = NSA_HEAD_DIM ** -0.5 * LOG2E
    for h in range(H):
        seg = r[:, (h // 2) * LANES:(h // 2 + 1) * LANES]
        if h % 2:
            seg = pltpu.roll(seg, NSA_HEAD_DIM, 1)
        qh_t = jnp.where(low, seg * scale, 0.0).T.astype(BF16)
        for c in range(tm // TQ):
            qt_ref[0, c, :, h * TQ:(h + 1) * TQ] = qh_t[:, c * TQ:(c + 1) * TQ]
    kseg = r[:, 256:384]
    blk = lax.shift_right_logical(lax.broadcasted_iota(jnp.int32, (tm, LANES), 0) + pl.program_id(1) * tm, 6)
    ksa_ref[0, :, 0:LANES] = jnp.where(low, kseg, 0.0).astype(BF16)
    ksa_ref[0, :, LANES:2 * LANES] = jnp.where(blk == lane, 1.0, 0.0).astype(BF16)
    kw_ref[0] = r[:, 384:512].astype(BF16)
    vst_ref[0, 0] = jnp.where(low, pltpu.roll(nr0, NSA_HEAD_DIM, 1), 1.0).T.astype(BF16)
    vw_t = jnp.where(low, nr1, 1.0).T.astype(BF16)
    for c in range(tm // LANES):
        vwt_ref[0, c] = vw_t[:, c * LANES:(c + 1) * LANES]
    kcvc_ref[0] = jnp.where(low, pltpu.roll(kseg, NSA_HEAD_DIM, 1), pltpu.roll(nr0, NSA_HEAD_DIM, 1))


def _nsa_proj(pb, cos, sin):
    B, S, _ = pb.shape
    H, TQ = NSA_HEADS, NSA_TQ
    tm = NSA_TK
    tok = lambda w: pl.BlockSpec((1, tm, w), lambda b, j: (b, j, 0))
    return pl.pallas_call(
        _nsa_proj_kernel,
        grid=(B, S // tm),
        in_specs=[tok(WB), tok(LANES), tok(LANES)],
        out_specs=[pl.BlockSpec((1, tm // TQ, LANES, H * TQ), lambda b, j: (b, j, 0, 0)),
                   tok(2 * LANES),
                   pl.BlockSpec((1, 1, LANES, tm), lambda b, j: (b, j, 0, 0)),
                   tok(LANES),
                   pl.BlockSpec((1, tm // LANES, LANES, LANES), lambda b, j: (b, j, 0, 0)),
                   tok(LANES)],
        out_shape=[jax.ShapeDtypeStruct((B, S // TQ, LANES, H * TQ), BF16),
                   jax.ShapeDtypeStruct((B, S, 2 * LANES), BF16),
                   jax.ShapeDtypeStruct((B, S // tm, LANES, tm), BF16),
                   jax.ShapeDtypeStruct((B, S, LANES), BF16),
                   jax.ShapeDtypeStruct((B, S // LANES, LANES, LANES), BF16),
                   jax.ShapeDtypeStruct((B, S, LANES), F32)],
        compiler_params=_cparams(("parallel", "parallel")),
        name="nsa_proj",
    )(pb, cos, sin)


def _nsa_cmp_kernel(t_ref, ptop_ref, pbot_ref, wtop_ref, wbot_ref, b1_ref, w2k_ref, w2v_ref, kc_ref, vct_ref, sh_ref):
    n = t_ref.shape[1]
    t = t_ref[0]
    top = _dot((t + ptop_ref[...]).astype(BF16), wtop_ref[...])
    sh_ref[0:n, :] = _dot((t + pbot_ref[...]).astype(BF16), wbot_ref[...])
    sh_ref[n:n + 8, :] = jnp.zeros((8, sh_ref.shape[1]), F32)
    hid = jax.nn.gelu(top + sh_ref[pl.ds(1, n), :] + b1_ref[...])
    kc_ref[0] = _dot(hid[:, 0:CMP_HIDDEN].astype(BF16), w2k_ref[...]).astype(BF16)
    vc = _dot(hid[:, CMP_HIDDEN:2 * CMP_HIDDEN].astype(BF16), w2v_ref[...])
    low = lax.broadcasted_iota(jnp.int32, vc.shape, 1) < NSA_HEAD_DIM
    vct_ref[0] = jnp.where(low, vc, 1.0).T.astype(BF16)


def _nsa_compress(t16, ptop, pbot, wtop, wbot, b1, w2k, w2v):
    B, n, F = t16.shape
    full = lambda shape: pl.BlockSpec(shape, lambda b: (0,) * len(shape))
    return pl.pallas_call(
        _nsa_cmp_kernel,
        grid=(B,),
        in_specs=[pl.BlockSpec((1, n, F), lambda b: (b, 0, 0)), full((1, F)), full((1, F)),
                  full((F, 2 * CMP_HIDDEN)), full((F, 2 * CMP_HIDDEN)), full((1, 2 * CMP_HIDDEN)),
                  full((CMP_HIDDEN, LANES)), full((CMP_HIDDEN, LANES))],
        out_specs=[pl.BlockSpec((1, n, LANES), lambda b: (b, 0, 0)), pl.BlockSpec((1, LANES, n), lambda b: (b, 0, 0))],
        out_shape=[jax.ShapeDtypeStruct((B, n, LANES), BF16), jax.ShapeDtypeStruct((B, LANES, n), BF16)],
        scratch_shapes=[pltpu.VMEM((n + 8, 2 * CMP_HIDDEN), F32)],
        compiler_params=_cparams(("parallel",)),
        name="nsa_compress",
    )(t16, ptop, pbot, wtop, wbot, b1, w2k, w2v)


def _nsa_attn_kernel(qt_ref, g_ref, kc_ref, vct_ref, ovl_ref, ksa_ref, vst_ref, kw_ref, vwt_ref, o_ref, *scratch):
    H, TQ, TK = NSA_HEADS, NSA_TQ, NSA_TK
    M = H * TQ
    NB = MAX_SEL_BLOCKS
    s0 = pl.program_id(1) * TQ
    ncmp = kc_ref.shape[1]
    qt = qt_ref[0, 0]
    t_row = s0 + lax.broadcasted_iota(jnp.int32, (1, M), 1) % TQ

    cend = lax.broadcasted_iota(jnp.int32, (ncmp, M), 0) * CMP_STRIDE + (CMP_LEN - 1)
    s_c = jnp.where(cend <= t_row, _dot(kc_ref[0], qt), NEG_INF)
    m_c = jnp.max(s_c, axis=0, keepdims=True)
    e_c = jnp.exp2(s_c - m_c)
    a_c = _dot(vct_ref[0], e_c.astype(BF16))
    inv_c = jnp.where(m_c > 0.5 * NEG_INF, 1.0 / jnp.maximum(a_c[V_ONES:V_ONES + 1, :], 1e-30), 0.0)
    o_c = a_c * inv_c
    p_c = e_c * inv_c

    WK = WINDOW + TQ
    w0 = pl.multiple_of(jnp.maximum(s0 - WINDOW, 0), TQ)
    dist = t_row - (w0 + lax.broadcasted_iota(jnp.int32, (WK, M), 0))
    in_window = lax.bitcast_convert_type(dist, jnp.uint32) < jnp.uint32(WINDOW)
    s_w = jnp.where(in_window, _dot(kw_ref[0, pl.ds(w0, WK), :], qt), NEG_INF)
    e_w = jnp.exp2(s_w - jnp.max(s_w, axis=0, keepdims=True)).astype(BF16)
    a_w = _dot(vwt_ref[0, w0 // LANES], e_w[0:LANES])
    for c in range(1, WK // LANES):
        a_w = a_w + _dot(vwt_ref[0, w0 // LANES + c], e_w[c * LANES:(c + 1) * LANES])
    o_w = _normalize(a_w)

    psum = p_c[:, 0:TQ] + p_c[:, TQ:2 * TQ] + p_c[:, 2 * TQ:3 * TQ] + p_c[:, 3 * TQ:4 * TQ]
    p_hi = psum.astype(BF16)
    p_lo = (psum - p_hi.astype(F32)).astype(BF16)
    imp = _dot(ovl_ref[...], p_hi) + _dot(ovl_ref[...], p_lo)
    jb = lax.broadcasted_iota(jnp.int32, (NB, TQ), 0)
    tq_l = s0 + lax.broadcasted_iota(jnp.int32, (NB, TQ), 1)
    cur = lax.shift_right_logical(tq_l, 6)
    forced = (jb == 0) | (jb == cur) | (jb == cur - 1)
    score = jnp.where(forced, -3e38, jnp.where(jb * SEL_LEN <= tq_l, imp, NEG_INF))
    jbf = jb.astype(F32)
    sel = jnp.where(forced, 1.0, 0.0)
    for _ in range(SEL_TOPN - 3):
        best = jnp.max(score, axis=0, keepdims=True)
        first = jnp.min(jnp.where(score == best, jbf, float(NB)), axis=0, keepdims=True)
        hit = jbf == first
        sel = jnp.where(hit, 1.0, sel)
        score = jnp.where(hit, -3e38, score)
    bias = ((sel - 1.0) * (-NEG_INF)).astype(BF16)
    q_aug = jnp.concatenate([qt, jnp.concatenate([bias] * H, axis=1)], axis=0)

    NC = NSA_CHAINS
    CW = M // NC
    m_refs, acc_refs, buf_a, buf_b = (scratch[i * NC:(i + 1) * NC] for i in range(4))
    n_full = s0 // TK
    q_cols = [q_aug[:, c * CW:(c + 1) * CW] for c in range(NC)]

    def scores(j):
        k = ksa_ref[0, pl.ds(pl.multiple_of(j * TK, TK), TK), :]
        return [_dot(k, qc) for qc in q_cols]

    def values(j):
        return [vst_ref[0, j]] * NC

    kpos = n_full * TK + lax.broadcasted_iota(jnp.int32, (TK, CW), 0)
    _flash_loop(n_full, scores, values, kpos <= t_row[:, 0:CW], buf_a, buf_b, m_refs, acc_refs)
    o_s = jnp.concatenate([_normalize(acc_refs[c][...]) for c in range(NC)], axis=1)

    g = jax.nn.sigmoid(g_ref[0]).T
    outs = []
    for h in range(H):
        cols = slice(h * TQ, (h + 1) * TQ)
        r0 = NSA_HEAD_DIM + 3 * h
        o_h = (g[r0:r0 + 1, :] * o_c[0:NSA_HEAD_DIM, cols] + g[r0 + 1:r0 + 2, :] * o_s[0:NSA_HEAD_DIM, cols]
               + g[r0 + 2:r0 + 3, :] * o_w[0:NSA_HEAD_DIM, cols])
        outs.append(o_h)
    o_ref[0] = jnp.concatenate(outs, axis=0).T


def _nsa_attn(qt, pb, kc, vct, ovl, ksa, vst, kw, vwt):
    B, S, _ = ksa.shape
    H, TQ, TK = NSA_HEADS, NSA_TQ, NSA_TK
    ncmp = kc.shape[1]
    per_b = lambda *shape: pl.BlockSpec((1,) + shape, lambda b, i: (b,) + (0,) * len(shape))
    return pl.pallas_call(
        _nsa_attn_kernel,
        grid=(B, S // TQ),
        in_specs=[pl.BlockSpec((1, 1, LANES, H * TQ), lambda b, i: (b, i, 0, 0)),
                  pl.BlockSpec((1, TQ, LANES), lambda b, i: (b, i, WB // LANES - 1)),
                  per_b(ncmp, LANES), per_b(LANES, ncmp),
                  pl.BlockSpec((MAX_SEL_BLOCKS, ncmp), lambda b, i: (0, 0)),
                  per_b(S, 2 * LANES), per_b(S // TK, LANES, TK), per_b(S, LANES), per_b(S // LANES, LANES, LANES)],
        out_specs=pl.BlockSpec((1, TQ, H * NSA_HEAD_DIM), lambda b, i: (b, i, 0)),
        out_shape=jax.ShapeDtypeStruct((B, S, H * NSA_HEAD_DIM), F32),
        scratch_shapes=([pltpu.VMEM((1, H * TQ // NSA_CHAINS), F32)] * NSA_CHAINS
                        + [pltpu.VMEM((LANES, H * TQ // NSA_CHAINS), F32)] * NSA_CHAINS
                        + [pltpu.VMEM((TK, H * TQ // NSA_CHAINS), F32)] * (2 * NSA_CHAINS)),
        compiler_params=_cparams(("parallel", "arbitrary")),
        name="nsa_attn",
    )(qt, pb, kc, vct, ovl, ksa, vst, kw, vwt)


def _nsa_overlap(ncmp):
    n = np.arange(ncmp)[None, :] * CMP_STRIDE
    s = np.arange(MAX_SEL_BLOCKS)[:, None] * SEL_LEN
    real = np.arange(ncmp)[None, :] < ncmp - 1
    ovl = (n < s + SEL_LEN) & (n + CMP_LEN > s) & real
    return jnp.asarray(ovl.astype(np.float32), dtype=BF16)


def _mix_ffn_kernel(ya_ref, yb_ref, yc_ref, yd_ref, x_ref, gn_ref, wo_ref, g_ref, wgu_ref, wd_ref, fn_ref, o_ref, *,
                    final_norm):
    x = x_ref[...]
    for g, y_ref in enumerate((ya_ref, yb_ref, yc_ref, yd_ref)):
        sl = slice(g * GROUP_WIDTH, (g + 1) * GROUP_WIDTH)
        y = (_rms(y_ref[...], GROUP_WIDTH) * gn_ref[:, sl]).astype(BF16)
        x = x + _dot(y, wo_ref[sl, :])
    o_ref[...] = x
    x = o_ref[...]
    h = (_rms(x, D_MODEL) * g_ref[...]).astype(BF16)
    acc = x
    for c in range(D_FF // FF_CHUNK):
        lo = c * FF_CHUNK
        gate = _dot(h, wgu_ref[:, lo:lo + FF_CHUNK])
        up = _dot(h, wgu_ref[:, D_FF + lo:D_FF + lo + FF_CHUNK])
        acc = acc + _dot((jax.nn.silu(gate) * up).astype(BF16), wd_ref[lo:lo + FF_CHUNK, :])
    if final_norm:
        acc = _rms(acc, D_MODEL) * fn_ref[...]
    o_ref[...] = acc


def _mix_ffn(ys, x2, gn, w_out_b, gain, wgu_b, wd_b, fnorm, layer, final_norm, tm=512):
    T = x2.shape[0]
    W = GROUP_WIDTH
    row = pl.BlockSpec((1, D_MODEL), lambda i: (0, 0))
    resident = lambda shape: pl.BlockSpec((None,) + shape, lambda i: (layer, 0, 0))
    return pl.pallas_call(
        functools.partial(_mix_ffn_kernel, final_norm=final_norm),
        grid=(T // tm,),
        in_specs=[pl.BlockSpec((tm, W), lambda i: (i, 0))] * 4
        + [pl.BlockSpec((tm, D_MODEL), lambda i: (i, 0)), row, resident((D_MODEL, D_MODEL)), row,
           resident((D_MODEL, 2 * D_FF)), resident((D_FF, D_MODEL)), row],
        out_specs=pl.BlockSpec((tm, D_MODEL), lambda i: (i, 0)),
        out_shape=jax.ShapeDtypeStruct((T, D_MODEL), F32),
        compiler_params=_cparams(("parallel",)),
        name="mix_ffn",
    )(*ys, x2, gn.reshape(1, D_MODEL), w_out_b, gain.reshape(1, D_MODEL), wgu_b, wd_b, fnorm.reshape(1, D_MODEL))


def _gather_cols(w, layout):
    parts = []
    for name, width in layout:
        if name is None:
            parts.append(jnp.zeros(w.shape[:-1] + (width,), w.dtype))
        else:
            o, n = _OFF[name]
            assert n == width
            parts.append(w[..., o:o + n])
    return jnp.concatenate(parts, axis=-1)


def _block_diag(w):
    h, d, _ = w.shape
    out = jnp.zeros((h * d, h * d), w.dtype)
    for i in range(h):
        out = out.at[i * d:(i + 1) * d, i * d:(i + 1) * d].set(w[i])
    return out


def _pad_to(a, shape):
    return jnp.pad(a, [(0, s - d) for s, d in zip(shape, a.shape)])


def _mla_weights(w_uq, w_ukv):
    H = MLA_HEADS
    wq = w_uq.reshape(MLA_Q_RANK, H, MLA_QK_DIM)
    wq = _pad_to(wq, (256, H, LANES)).reshape(256, H * LANES)
    wkv = w_ukv.reshape(MLA_KV_RANK, H, MLA_NOPE_DIM + MLA_V_DIM)
    wk = _pad_to(wkv[:, :, :MLA_NOPE_DIM], (MLA_KV_RANK, H, LANES)).reshape(MLA_KV_RANK, H * LANES)
    wv = _pad_to(wkv[:, :, MLA_NOPE_DIM:], (MLA_KV_RANK, H, LANES)).reshape(MLA_KV_RANK, H * LANES)
    return wq.astype(BF16), jnp.concatenate([wk, wv], axis=1).astype(BF16)


def _cmp_weights(cmp_pos, cmp_w1, cmp_b1, cmp_w2):
    half = CMP_LEN // 2
    Dh = NSA_HEAD_DIM
    w1 = cmp_w1.reshape(2, CMP_LEN, Dh, CMP_HIDDEN)

    def rows(part):
        wk = w1[0, part * half:(part + 1) * half]
        wv = w1[1, part * half:(part + 1) * half]
        z = jnp.zeros_like(wk)
        top = jnp.concatenate([wk, z], axis=-1)
        bot = jnp.concatenate([z, wv], axis=-1)
        return jnp.concatenate([top, bot], axis=1).reshape(half * 2 * Dh, 2 * CMP_HIDDEN)

    def pos_row(part):
        p = jnp.concatenate([cmp_pos[0, part * half:(part + 1) * half], cmp_pos[1, part * half:(part + 1) * half]],
                            axis=-1)
        return p.reshape(1, half * 2 * Dh)

    b1 = jnp.concatenate([cmp_b1[0], cmp_b1[1]]).reshape(1, 2 * CMP_HIDDEN)
    w2k = _pad_to(cmp_w2[0], (CMP_HIDDEN, LANES)).astype(BF16)
    w2v = _pad_to(cmp_w2[1], (CMP_HIDDEN, LANES)).astype(BF16)
    return pos_row(0), pos_row(1), rows(0).astype(BF16), rows(1).astype(BF16), b1, w2k, w2v


def kernel(x, positions, norm_mix, w_in, conv_w, conv_b, lru_wa, lru_ba, lru_wx, lru_bx, lru_lambda, cmp_pos, cmp_w1, cmp_b1, cmp_w2, gla_wg2, gla_bg2, gla_norm, mla_q_norm, mla_kv_norm, mla_w_uq, mla_w_ukv, group_norm, w_out, norm_ffn, w_gate_up, w_down, final_norm):
    B, S, D = x.shape
    assert D == D_MODEL and S % (CMP_STRIDE * 8) == 0 and S // SEL_LEN <= MAX_SEL_BLOCKS and S >= WINDOW + NSA_TQ
    depth = w_in.shape[0]
    T = B * S
    pos3 = positions.reshape(B, S, 1)
    ncmp = S // CMP_STRIDE
    ovl = _nsa_overlap(ncmp)
    cos_n, sin_n, cos_m, sin_m = _rope_tables(pos3)
    x2 = x.reshape(T, D)
    w_out_b, wgu_b, wd_b = w_out.astype(BF16), w_gate_up.astype(BF16), w_down.astype(BF16)
    w_p = jnp.concatenate([_gather_cols(w_in.astype(BF16), lay) for lay in (_LAYOUT_A, _LAYOUT_B, _LAYOUT_C, _LAYOUT_D)],
                          axis=-1)
    for l in range(depth):
        pa, pb, pc, pd = _inproj(x2, norm_mix[l], w_p, l)
        pa, pb, pc, pd = (p.reshape(B, S, -1) for p in (pa, pb, pc, pd))

        wg = jnp.concatenate([_block_diag(lru_wa[l]), _block_diag(lru_wx[l])], axis=1).astype(BF16)
        bg = jnp.concatenate([lru_ba[l], lru_bx[l]]).reshape(1, 2 * GROUP_WIDTH)
        y_a = _rglru(pa, conv_w[l], conv_b[l], wg, bg, lru_lambda[l])

        qt_n, ksa, vst, kw, vwt, kcvc = _nsa_proj(pb, cos_n, sin_n)
        t16 = kcvc.reshape(B, ncmp, CMP_STRIDE * 2 * NSA_HEAD_DIM)
        kc, vct = _nsa_compress(t16, *_cmp_weights(cmp_pos[l], cmp_w1[l], cmp_b1[l], cmp_w2[l]))
        y_b = _nsa_attn(qt_n, pb, kc, vct, ovl, ksa, vst, kw, vwt)

        wg2_p = _pad_to(gla_wg2[l], (LANES, GLA_HEADS * GLA_DK))
        y_c = _gla(pc, wg2_p, gla_bg2[l], jnp.tile(gla_norm[l], GLA_HEADS))

        wq_p, wkv_p = _mla_weights(mla_w_uq[l], mla_w_ukv[l])
        qt_m, k_m, vt_m = _mla_proj(pd, cos_m, sin_m, _pad_to(mla_q_norm[l], (256,)).reshape(1, 256), mla_kv_norm[l],
                                    wq_p, wkv_p)
        y_d = _mla_attn(qt_m, k_m, vt_m)

        ys = [y.reshape(T, GROUP_WIDTH) for y in (y_a, y_b, y_c, y_d)]
        x2 = _mix_ffn(ys, x2, group_norm[l], w_out_b, norm_ffn[l], wgu_b, wd_b, final_norm, l,
                      final_norm=(l == depth - 1))
    return x2.reshape(B, S, D)
```

```python
import functools

import numpy as np
import jax
import jax.numpy as jnp
from jax import lax
from jax.experimental import pallas as pl
from jax.experimental.pallas import tpu as pltpu

F32 = jnp.float32
BF16 = jnp.bfloat16

D_MODEL = 1024
GROUP_WIDTH = 256
ROPE_THETA = 10000.0
NORM_EPS = 1e-6
NEG_INF = -1e30
FORCE_SCORE = 1e9

LRU_C = 8.0
CONV_WIDTH = 4

NSA_HEADS = 4
NSA_HEAD_DIM = 64
CMP_LEN = 32
CMP_STRIDE = 16
CMP_HIDDEN = 256
SEL_LEN = 64
SEL_TOPN = 16
WINDOW = 512
NSA_TQ = 256
NSA_TK = 512
NSA_CHAINS = 2
MAX_SEL_BLOCKS = 128

GLA_HEADS = 4
GLA_DV = 64
GLA_DK = 32
GLA_GATE_RANK = 16
GLA_TAU = 16.0
GLA_TILE = 256
GLA_SUB = 16

MLA_HEADS = 4
MLA_V_DIM = 64
MLA_NOPE_DIM = 64
MLA_ROPE_DIM = 32
MLA_QK_DIM = 96
MLA_Q_RANK = 192
MLA_KV_RANK = 128
MLA_TQ = 512
MLA_TK = 512

D_FF = 2816
FF_CHUNK = 256

LANES = 128
SUBLANES = 8
V_ONES = 64
LOG2E = 1.4426950408889634
VMEM_LIMIT = 56 * 1024 * 1024

_OFF = {}
_o = 0
for _n, _w in (("a_x", 256), ("a_gate", 256), ("b_q", 256), ("k_c", 64), ("v_c", 64), ("k_s", 64), ("v_s", 64),
               ("k_w", 64), ("v_w", 64), ("b_gate", 12), ("c_q", 128), ("c_k", 128), ("c_v", 256), ("c_glr", 16),
               ("c_og", 256), ("d_cq", 192), ("d_ckv", 128), ("d_kr", 32)):
    _OFF[_n] = (_o, _w)
    _o += _w
IN_COLS = _o
_LAYOUT_A = (("a_x", 256), ("a_gate", 256))
_LAYOUT_B = (("b_q", 256), ("k_s", 64), ("k_c", 64), ("k_w", 64), (None, 64),
             ("v_c", 64), ("v_s", 64), ("v_w", 64), ("b_gate", 12), (None, 52))
_LAYOUT_C = (("c_q", 128), ("c_k", 128), ("c_v", 256), ("c_og", 256), ("c_glr", 16), (None, 112))
_LAYOUT_D = (("d_cq", 192), (None, 64), ("d_ckv", 128), (None, 64), ("d_kr", 32), (None, 32))
WA, WB, WC, WD = 512, 768, 896, 512


def _cparams(sem):
    return pltpu.CompilerParams(dimension_semantics=sem, vmem_limit_bytes=VMEM_LIMIT)


def _dot(a, b):
    return jnp.dot(a, b, preferred_element_type=F32)


def _dot_nt(a, b):
    return lax.dot_general(a, b, (((1,), (1,)), ((), ())), preferred_element_type=F32)


def _dot_tn(a, b):
    return lax.dot_general(a, b, (((0,), (0,)), ((), ())), preferred_element_type=F32)


def _split2(x):
    hi = x.astype(BF16)
    return hi, (x - hi.astype(F32)).astype(BF16)


def _dot_split(a, b, lhs_split=False):
    if lhs_split:
        hi, lo = _split2(a)
        return _dot(hi, b) + _dot(lo, b)
    hi, lo = _split2(b)
    return _dot(a, hi) + _dot(a, lo)


def _dot_x3(a, b):
    a_hi, a_lo = _split2(a)
    b_hi, b_lo = _split2(b)
    return _dot(a_hi, b_hi) + (_dot(a_hi, b_lo) + _dot(a_lo, b_hi))


def _rms(x, width):
    return x * lax.rsqrt(jnp.sum(x * x, axis=-1, keepdims=True) / width + NORM_EPS)


def _inproj_kernel(x_ref, g_ref, w_ref, oa_ref, ob_ref, oc_ref, od_ref):
    h = (_rms(x_ref[...], D_MODEL) * g_ref[...]).astype(BF16)
    off = 0
    for o_ref in (oa_ref, ob_ref, oc_ref, od_ref):
        w = o_ref.shape[1]
        o_ref[...] = _dot(h, w_ref[:, off:off + w])
        off += w


def _inproj(x2, gain, w_p, layer, tm=512):
    T = x2.shape[0]
    ntot = w_p.shape[2]
    return pl.pallas_call(
        _inproj_kernel,
        grid=(T // tm,),
        in_specs=[pl.BlockSpec((tm, D_MODEL), lambda i: (i, 0)),
                  pl.BlockSpec((1, D_MODEL), lambda i: (0, 0)),
                  pl.BlockSpec((None, D_MODEL, ntot), lambda i: (layer, 0, 0))],
        out_specs=[pl.BlockSpec((tm, w), lambda i: (i, 0)) for w in (WA, WB, WC, WD)],
        out_shape=[jax.ShapeDtypeStruct((T, w), F32) for w in (WA, WB, WC, WD)],
        compiler_params=_cparams(("parallel",)),
        name="inproj",
    )(x2, gain.reshape(1, D_MODEL), w_p)


def _rglru_kernel(pa_ref, cw_ref, cb_ref, wg_ref, bg_ref, lam_ref, o_ref, xbuf, a_s, u_s, hp_s, hlast):
    ts = pa_ref.shape[1]
    W = GROUP_WIDTH

    @pl.when(pl.program_id(1) == 0)
    def _():
        xbuf[0:8, :] = jnp.zeros((8, W), F32)
        hlast[...] = jnp.zeros_like(hlast)

    xbuf[8:8 + ts, :] = pa_ref[0, :, 0:W]
    xc = cb_ref[...]
    for k in range(CONV_WIDTH):
        lo = 8 - (CONV_WIDTH - 1) + k
        xc = xc + cw_ref[k:k + 1, :] * xbuf[lo:lo + ts, :]
    xbuf[0:8, :] = xbuf[ts:ts + 8, :]
    gates = _dot(xc.astype(BF16), wg_ref[...]) + bg_ref[...]
    r = jax.nn.sigmoid(gates[:, 0:W])
    i = jax.nn.sigmoid(gates[:, W:2 * W])
    log_a = (-LRU_C) * r * jax.nn.softplus(-lam_ref[...])
    a = jnp.exp(log_a)
    u = jnp.sqrt(-jnp.tanh(log_a) * (a * a + 1.0)) * (i * xc)
    G = ts // SUBLANES
    row8 = lax.broadcasted_iota(jnp.int32, (ts, W), 0) % SUBLANES
    for d in (1, 2, 4):
        keep = row8 >= d
        u = u + a * jnp.where(keep, pltpu.roll(u, d, 0), 0.0)
        a = a * jnp.where(keep, pltpu.roll(a, d, 0), 1.0)
    a_s[...] = a
    u_s[...] = u

    def body(g, h):
        hp_s[pl.ds(g, 1), :] = h
        r = g * SUBLANES + (SUBLANES - 1)
        return a_s[pl.ds(r, 1), :] * h + u_s[pl.ds(r, 1), :]

    hlast[...] = lax.fori_loop(0, G, body, hlast[...], unroll=8)
    h_in = jnp.broadcast_to(hp_s[...].reshape(G, 1, W), (G, SUBLANES, W)).reshape(ts, W)
    o_ref[0] = (a * h_in + u) * jax.nn.gelu(pa_ref[0, :, W:2 * W])


def _rglru(pa, conv_w, conv_b, wg, bg, lam, ts=512):
    B, S, _ = pa.shape
    W = GROUP_WIDTH
    full = lambda shape: pl.BlockSpec(shape, lambda b, j: (0,) * len(shape))
    return pl.pallas_call(
        _rglru_kernel,
        grid=(B, S // ts),
        in_specs=[pl.BlockSpec((1, ts, WA), lambda b, j: (b, j, 0)),
                  full((CONV_WIDTH, W)), full((1, W)), full((W, 2 * W)), full((1, 2 * W)), full((1, W))],
        out_specs=pl.BlockSpec((1, ts, W), lambda b, j: (b, j, 0)),
        out_shape=jax.ShapeDtypeStruct((B, S, W), F32),
        scratch_shapes=[pltpu.VMEM((ts + 8, W), F32), pltpu.VMEM((ts, W), F32), pltpu.VMEM((ts, W), F32),
                        pltpu.VMEM((ts // SUBLANES, W), F32), pltpu.VMEM((1, W), F32)],
        compiler_params=_cparams(("parallel", "arbitrary")),
        name="rglru",
    )(pa, conv_w, conv_b.reshape(1, W), wg, bg, lam.reshape(1, W))


def _gla_kernel(pc_ref, wg2_ref, bg2_ref, gn_ref, lcs_ref, hsum_ref, gmean_ref, smask_ref, o_ref, st_ref):
    TT, SUB = GLA_TILE, GLA_SUB
    NB = TT // SUB
    KW = GLA_HEADS * GLA_DK
    VW = GLA_HEADS * GLA_DV

    @pl.when(pl.program_id(1) == 0)
    def _():
        st_ref[...] = jnp.zeros_like(st_ref)

    q = pc_ref[0, :, 0:KW] * (GLA_DK ** -0.5)
    k = pc_ref[0, :, KW:2 * KW]
    v = pc_ref[0, :, 2 * KW:2 * KW + VW]
    og = pc_ref[0, :, 2 * KW + VW:2 * KW + 2 * VW]
    glr = pc_ref[0, :, 2 * KW + 2 * VW:2 * KW + 2 * VW + LANES]
    log_a = jax.nn.log_sigmoid(_dot_x3(glr, wg2_ref[...]) + bg2_ref[...]) * (1.0 / GLA_TAU)
    cums = _dot_split(lcs_ref[...], log_a)
    b = cums[0:TT]
    bl = cums[TT:2 * TT]

    q3 = q.reshape(NB, SUB, KW)
    k3 = k.reshape(NB, SUB, KW)
    b3 = b.reshape(NB, SUB, KW)
    v3 = v.reshape(NB, SUB, VW)
    row = lax.broadcasted_iota(jnp.int32, (NB, SUB, KW), 1)
    terms = []
    for j in range(SUB):
        kj = jnp.broadcast_to(k3[:, j:j + 1, :], (NB, SUB, KW))
        bj = jnp.broadcast_to(b3[:, j:j + 1, :], (NB, SUB, KW))
        e = q3 * kj * jnp.exp(jnp.where(row >= j, b3 - bj, NEG_INF))
        terms.append(e.reshape(TT, KW))
    e_all = jnp.concatenate(terms, axis=0)
    attn = _dot(e_all.astype(BF16), hsum_ref[...])
    o = jnp.zeros((NB, SUB, VW), F32)
    for j in range(SUB):
        vj = jnp.broadcast_to(v3[:, j:j + 1, :], (NB, SUB, VW))
        o = o + attn[j * TT:(j + 1) * TT, :].reshape(NB, SUB, VW) * vj
    o = o.reshape(TT, VW)

    qd = (q * jnp.exp(b)).astype(BF16)
    kd = (k * jnp.exp(bl - b)).astype(BF16)
    dec = jnp.exp(bl)
    vb = v.astype(BF16)
    blocks = [slice(n * SUB, (n + 1) * SUB) for n in range(NB)]
    upds = [_dot_tn(vb[rows], kd[rows]) * smask_ref[...] for rows in blocks]
    st = st_ref[...]
    sts = []
    for n in range(NB):
        sts.append(st.astype(BF16))
        st = st * dec[n * SUB:n * SUB + 1, :] + upds[n]
    st_ref[...] = st
    o = o + jnp.concatenate([_dot_nt(qd[rows], s_n) for rows, s_n in zip(blocks, sts)], axis=0)

    ms = _dot_split(o * o, gmean_ref[...], lhs_split=True)
    o_ref[0] = o * lax.rsqrt(ms + NORM_EPS) * gn_ref[...] * jax.nn.silu(og)


def _gla_consts():
    TT, SUB = GLA_TILE, GLA_SUB
    KW, VW = GLA_HEADS * GLA_DK, GLA_HEADS * GLA_DV
    i = np.arange(TT)
    same = (i[:, None] // SUB) == (i[None, :] // SUB)
    lcum = (same & (i[None, :] <= i[:, None])).astype(np.float32)
    lsum = same.astype(np.float32)
    hk = np.arange(KW) // GLA_DK
    hv = np.arange(VW) // GLA_DV
    hsum = (hk[:, None] == hv[None, :]).astype(np.float32)
    gmean = (hv[:, None] == hv[None, :]).astype(np.float32) / GLA_DV
    smask = (hv[:, None] == hk[None, :]).astype(np.float32)
    return (jnp.asarray(np.concatenate([lcum, lsum], axis=0), dtype=BF16), jnp.asarray(hsum, dtype=BF16),
            jnp.asarray(gmean, dtype=BF16), jnp.asarray(smask))


def _gla(pc, wg2_p, bg2, gn_t):
    B, S, _ = pc.shape
    TT = GLA_TILE
    KW, VW = GLA_HEADS * GLA_DK, GLA_HEADS * GLA_DV
    lcs, hsum, gmean, smask = _gla_consts()
    full = lambda shape: pl.BlockSpec(shape, lambda b, j: (0,) * len(shape))
    return pl.pallas_call(
        _gla_kernel,
        grid=(B, S // TT),
        in_specs=[pl.BlockSpec((1, TT, WC), lambda b, j: (b, j, 0)),
                  full((LANES, KW)), full((1, KW)), full((1, VW)), full((2 * TT, TT)),
                  full((KW, VW)), full((VW, VW)), full((VW, KW))],
        out_specs=pl.BlockSpec((1, TT, VW), lambda b, j: (b, j, 0)),
        out_shape=jax.ShapeDtypeStruct((B, S, VW), F32),
        scratch_shapes=[pltpu.VMEM((VW, KW), F32)],
        compiler_params=_cparams(("parallel", "arbitrary")),
        name="gla",
    )(pc, wg2_p, bg2.reshape(1, KW), gn_t.reshape(1, VW), lcs, hsum, gmean, smask)


def _rope_lanes(x, cos, sin_signed, half):
    lane = lax.broadcasted_iota(jnp.int32, cos.shape, 1)
    lo = (lane % (2 * half)) < half
    outs = []
    for c in range(x.shape[1] // LANES):
        xs = x[:, c * LANES:(c + 1) * LANES]
        rot = jnp.where(lo, pltpu.roll(xs, LANES - half, 1), pltpu.roll(xs, half, 1))
        outs.append(xs * cos + rot * sin_signed)
    return outs[0] if len(outs) == 1 else jnp.concatenate(outs, axis=1)


def _rope_table_kernel(pos_ref, invf_ref, cn_ref, sn_ref, cm_ref, sm_ref):
    ang = pos_ref[0].astype(F32) * invf_ref[...]
    lane = lax.broadcasted_iota(jnp.int32, ang.shape, 1)
    hn, hm = NSA_HEAD_DIM // 2, MLA_ROPE_DIM // 2
    for src, n_ref, m_ref, is_sin in ((jnp.cos(ang), cn_ref, cm_ref, False), (jnp.sin(ang), sn_ref, sm_ref, True)):
        t = jnp.where(lane < hn, src, pltpu.roll(src, hn, 1))
        t = jnp.where(lane < 2 * hn, t, pltpu.roll(t, 2 * hn, 1))
        m = jnp.where(lane < MLA_NOPE_DIM + hm, pltpu.roll(src, MLA_NOPE_DIM - hn, 1),
                      pltpu.roll(src, MLA_NOPE_DIM - hn + hm, 1))
        rope_m = (lane >= MLA_NOPE_DIM) & (lane < MLA_QK_DIM)
        if is_sin:
            n_ref[0] = jnp.where(lane % (2 * hn) < hn, -t, t)
            m_ref[0] = jnp.where(rope_m, jnp.where(lane % (2 * hm) < hm, -m, m), 0.0)
        else:
            n_ref[0] = t
            m_ref[0] = jnp.where(rope_m, m, 1.0)


def _rope_tables(pos3, tm=1024):
    B, S, _ = pos3.shape
    inv_n = ROPE_THETA ** (-jnp.arange(0, NSA_HEAD_DIM, 2, dtype=F32) / NSA_HEAD_DIM)
    inv_m = ROPE_THETA ** (-jnp.arange(0, MLA_ROPE_DIM, 2, dtype=F32) / MLA_ROPE_DIM)
    invf = jnp.concatenate([inv_n, inv_m, jnp.zeros((LANES - inv_n.shape[0] - inv_m.shape[0],), F32)]).reshape(1, LANES)
    spec = pl.BlockSpec((1, tm, LANES), lambda b, j: (b, j, 0))
    shape = jax.ShapeDtypeStruct((B, S, LANES), F32)
    return pl.pallas_call(
        _rope_table_kernel,
        grid=(B, S // tm),
        in_specs=[pl.BlockSpec((1, tm, 1), lambda b, j: (b, j, 0)), pl.BlockSpec((1, LANES), lambda b, j: (0, 0))],
        out_specs=[spec] * 4,
        out_shape=[shape] * 4,
        compiler_params=_cparams(("parallel", "parallel")),
        name="rope_tables",
    )(pos3, invf)


def _mla_proj_kernel(pd_ref, cos_ref, sin_ref, qn_ref, kvn_ref, wq_ref, wkv_ref, qt_ref, k_ref, vt_ref):
    H = MLA_HEADS
    tm = pd_ref.shape[1]
    cos, sin = cos_ref[0], sin_ref[0]
    cq = (_rms(pd_ref[0, :, 0:256], MLA_Q_RANK) * qn_ref[...]).astype(BF16)
    ckv = (_rms(pd_ref[0, :, 256:384], MLA_KV_RANK) * kvn_ref[...]).astype(BF16)
    kr = _rope_lanes(pd_ref[0, :, 384:512], cos, sin, MLA_ROPE_DIM // 2)
    q_all = _dot(cq, wq_ref[...])
    kv_all = _dot(ckv, wkv_ref[...])
    low = lax.broadcasted_iota(jnp.int32, (tm, LANES), 1) < MLA_V_DIM
    for h in range(H):
        qh = _rope_lanes(q_all[:, h * LANES:(h + 1) * LANES], cos, sin, MLA_ROPE_DIM // 2)
        qt_ref[0, h, 0] = (qh * (MLA_QK_DIM ** -0.5 * LOG2E)).T.astype(BF16)
        k_ref[0, h] = (kv_all[:, h * LANES:(h + 1) * LANES] + kr).astype(BF16)
        vt_ref[0, h, 0] = jnp.where(low, kv_all[:, (H + h) * LANES:(H + h + 1) * LANES], 1.0).T.astype(BF16)


def _mla_proj(pd, cos, sin, qn_p, kvn, wq_p, wkv_p):
    B, S, _ = pd.shape
    H = MLA_HEADS
    tm = MLA_TK
    assert MLA_TQ == tm
    full = lambda shape: pl.BlockSpec(shape, lambda b, j: (0,) * len(shape))
    tspec = pl.BlockSpec((1, H, 1, LANES, tm), lambda b, j: (b, 0, j, 0, 0))
    tshape = jax.ShapeDtypeStruct((B, H, S // tm, LANES, tm), BF16)
    return pl.pallas_call(
        _mla_proj_kernel,
        grid=(B, S // tm),
        in_specs=[pl.BlockSpec((1, tm, WD), lambda b, j: (b, j, 0)),
                  pl.BlockSpec((1, tm, LANES), lambda b, j: (b, j, 0)),
                  pl.BlockSpec((1, tm, LANES), lambda b, j: (b, j, 0)),
                  full((1, 256)), full((1, LANES)), full((256, H * LANES)), full((LANES, 2 * H * LANES))],
        out_specs=[tspec, pl.BlockSpec((1, H, tm, LANES), lambda b, j: (b, 0, j, 0)), tspec],
        out_shape=[tshape, jax.ShapeDtypeStruct((B, H, S, LANES), BF16), tshape],
        compiler_params=_cparams(("parallel", "parallel")),
        name="mla_proj",
    )(pd, cos, sin, qn_p, kvn.reshape(1, LANES), wq_p, wkv_p)


def _softmax_update(s_refs, vts, m_refs, acc_refs, mask):
    ss = [r[...] for r in s_refs]
    if mask is not None:
        ss = [jnp.where(mask, s, NEG_INF) for s in ss]
    m_prevs = [r[...] for r in m_refs]
    m_news = [jnp.maximum(mp, jnp.max(s, axis=0, keepdims=True)) for mp, s in zip(m_prevs, ss)]
    ps = [jnp.exp2(s - mn).astype(BF16) for s, mn in zip(ss, m_news)]
    for vt1, m_ref, acc_ref, p, mp, mn in zip(vts, m_refs, acc_refs, ps, m_prevs, m_news):
        acc_ref[...] = jnp.exp2(mp - mn) * acc_ref[...] + _dot(vt1, p)
        m_ref[...] = mn


def _flash_loop(n_full, scores, values, mask, buf_a, buf_b, m_refs, acc_refs):
    for r in m_refs:
        r[...] = jnp.full(r.shape, NEG_INF, F32)
    for r in acc_refs:
        r[...] = jnp.zeros(r.shape, F32)

    def fill(bufs, j):
        for r, sc in zip(bufs, scores(j)):
            r[...] = sc

    fill(buf_a, 0)

    def body(jj, carry):
        j = 2 * jj
        fill(buf_b, j + 1)
        _softmax_update(buf_a, values(j), m_refs, acc_refs, None)
        fill(buf_a, j + 2)
        _softmax_update(buf_b, values(j + 1), m_refs, acc_refs, None)
        return carry

    lax.fori_loop(0, n_full // 2, body, 0)

    @pl.when(n_full % 2 == 1)
    def _():
        fill(buf_b, n_full)
        _softmax_update(buf_a, values(n_full - 1), m_refs, acc_refs, None)
        _softmax_update(buf_b, values(n_full), m_refs, acc_refs, mask)

    @pl.when(n_full % 2 == 0)
    def _():
        _softmax_update(buf_a, values(n_full), m_refs, acc_refs, mask)


def _normalize(acc):
    return acc / jnp.maximum(acc[V_ONES:V_ONES + 1, :], 1e-30)


def _mla_attn_kernel(qt_ref, k_ref, vt_ref, o_ref, *scratch):
    tq, tk = MLA_TQ, MLA_TK
    H = MLA_HEADS
    m_refs, acc_refs, buf_a, buf_b = (scratch[i * H:(i + 1) * H] for i in range(4))
    s0 = pl.program_id(1) * tq
    n_full = s0 // tk

    def scores(j):
        off = pl.multiple_of(j * tk, tk)
        return [_dot(k_ref[0, h, pl.ds(off, tk), :], qt_ref[0, h, 0]) for h in range(H)]

    def values(j):
        return [vt_ref[0, h, j] for h in range(H)]

    kpos = n_full * tk + lax.broadcasted_iota(jnp.int32, (tk, tq), 0)
    t = s0 + lax.broadcasted_iota(jnp.int32, (tk, tq), 1)
    _flash_loop(n_full, scores, values, kpos <= t, buf_a, buf_b, m_refs, acc_refs)
    o_t = jnp.concatenate([_normalize(acc_refs[h][...])[0:MLA_V_DIM, :] for h in range(H)], axis=0)
    o_ref[0] = o_t.T


def _mla_attn(qt, k, vt):
    B, H, S, _ = k.shape
    tq, tk = MLA_TQ, MLA_TK
    return pl.pallas_call(
        _mla_attn_kernel,
        grid=(B, S // tq),
        in_specs=[pl.BlockSpec((1, H, 1, LANES, tq), lambda b, i: (b, 0, i, 0, 0)),
                  pl.BlockSpec((1, H, S, LANES), lambda b, i: (b, 0, 0, 0)),
                  pl.BlockSpec((1, H, S // tk, LANES, tk), lambda b, i: (b, 0, 0, 0, 0))],
        out_specs=pl.BlockSpec((1, tq, H * MLA_V_DIM), lambda b, i: (b, i, 0)),
        out_shape=jax.ShapeDtypeStruct((B, S, H * MLA_V_DIM), F32),
        scratch_shapes=([pltpu.VMEM((1, tq), F32)] * H + [pltpu.VMEM((LANES, tq), F32)] * H
                        + [pltpu.VMEM((tk, tq), F32)] * (2 * H)),
        compiler_params=_cparams(("parallel", "arbitrary")),
        name="mla_attn",
    )(qt, k, vt)


def _nsa_proj_kernel(pb_ref, cos_ref, sin_ref, qt_ref, ksa_ref, vst_ref, kw_ref, vwt_ref, kcvc_ref):
    tm = pb_ref.shape[1]
    H, TQ = NSA_HEADS, NSA_TQ
    r = _rope_lanes(pb_ref[0, :, 0:512], cos_ref[0], sin_ref[0], NSA_HEAD_DIM // 2)
    nr0 = pb_ref[0, :, 512:640]
    nr1 = pb_ref[0, :, 640:768]
    lane = lax.broadcasted_iota(jnp.int32, (tm, LANES), 1)
    low = lane < NSA_HEAD_DIM
    scale = NSA_HEAD_DIM ** -0.5 * LOG2E
    for h in range(H):
        seg = r[:, (h // 2) * LANES:(h // 2 + 1) * LANES]
        if h % 2:
            seg = pltpu.roll(seg, NSA_HEAD_DIM, 1)
        qh_t = jnp.where(low, seg * scale, 0.0).T.astype(BF16)
        for c in range(tm // TQ):
            qt_ref[0, c, :, h * TQ:(h + 1) * TQ] = qh_t[:, c * TQ:(c + 1) * TQ]
    kseg = r[:, 256:384]
    blk = lax.shift_right_logical(lax.broadcasted_iota(jnp.int32, (tm, LANES), 0) + pl.program_id(1) * tm, 6)
    ksa_ref[0, :, 0:LANES] = jnp.where(low, kseg, 0.0).astype(BF16)
    ksa_ref[0, :, LANES:2 * LANES] = jnp.where(blk == lane, 1.0, 0.0).astype(BF16)
    kw_ref[0] = r[:, 384:512].astype(BF16)
    vst_ref[0, 0] = jnp.where(low, pltpu.roll(nr0, NSA_HEAD_DIM, 1), 1.0).T.astype(BF16)
    vw_t = jnp.where(low, nr1, 1.0).T.astype(BF16)
    for c in range(tm // LANES):
        vwt_ref[0, c] = vw_t[:, c * LANES:(c + 1) * LANES]
    kcvc_ref[0] = jnp.where(low, pltpu.roll(kseg, NSA_HEAD_DIM, 1), pltpu.roll(nr0, NSA_HEAD_DIM, 1))


def _nsa_proj(pb, cos, sin):
    B, S, _ = pb.shape
    H, TQ = NSA_HEADS, NSA_TQ
    tm = NSA_TK
    tok = lambda w: pl.BlockSpec((1, tm, w), lambda b, j: (b, j, 0))
    return pl.pallas_call(
        _nsa_proj_kernel,
        grid=(B, S // tm),
        in_specs=[tok(WB), tok(LANES), tok(LANES)],
        out_specs=[pl.BlockSpec((1, tm // TQ, LANES, H * TQ), lambda b, j: (b, j, 0, 0)),
                   tok(2 * LANES),
                   pl.BlockSpec((1, 1, LANES, tm), lambda b, j: (b, j, 0, 0)),
                   tok(LANES),
                   pl.BlockSpec((1, tm // LANES, LANES, LANES), lambda b, j: (b, j, 0, 0)),
                   tok(LANES)],
        out_shape=[jax.ShapeDtypeStruct((B, S // TQ, LANES, H * TQ), BF16),
                   jax.ShapeDtypeStruct((B, S, 2 * LANES), BF16),
                   jax.ShapeDtypeStruct((B, S // tm, LANES, tm), BF16),
                   jax.ShapeDtypeStruct((B, S, LANES), BF16),
                   jax.ShapeDtypeStruct((B, S // LANES, LANES, LANES), BF16),
                   jax.ShapeDtypeStruct((B, S, LANES), F32)],
        compiler_params=_cparams(("parallel", "parallel")),
        name="nsa_proj",
    )(pb, cos, sin)


def _nsa_cmp_kernel(t_ref, ptop_ref, pbot_ref, wtop_ref, wbot_ref, b1_ref, w2k_ref, w2v_ref, kc_ref, vct_ref, sh_ref):
    n = t_ref.shape[1] // CMP_STRIDE
    F = 2 * NSA_HEAD_DIM
    top = jnp.zeros((n, 2 * CMP_HIDDEN), F32)
    bot = jnp.zeros((n, 2 * CMP_HIDDEN), F32)
    for l in range(CMP_STRIDE):
        t = t_ref[0, pl.ds(l, n, stride=CMP_STRIDE), :]
        sl = slice(l * F, (l + 1) * F)
        top = top + _dot((t + ptop_ref[:, sl]).astype(BF16), wtop_ref[sl, :])
        bot = bot + _dot((t + pbot_ref[:, sl]).astype(BF16), wbot_ref[sl, :])
    sh_ref[0:n, :] = bot
    sh_ref[n:n + 8, :] = jnp.zeros((8, sh_ref.shape[1]), F32)
    hid = jax.nn.gelu(top + sh_ref[pl.ds(1, n), :] + b1_ref[...])
    kc_ref[0] = _dot(hid[:, 0:CMP_HIDDEN].astype(BF16), w2k_ref[...]).astype(BF16)
    vc = _dot(hid[:, CMP_HIDDEN:2 * CMP_HIDDEN].astype(BF16), w2v_ref[...])
    low = lax.broadcasted_iota(jnp.int32, vc.shape, 1) < NSA_HEAD_DIM
    vct_ref[0] = jnp.where(low, vc, 1.0).T.astype(BF16)


def _nsa_compress(kcvc, ptop, pbot, wtop, wbot, b1, w2k, w2v):
    B, S, _ = kcvc.shape
    n, F = S // CMP_STRIDE, CMP_STRIDE * 2 * NSA_HEAD_DIM
    full = lambda shape: pl.BlockSpec(shape, lambda b: (0,) * len(shape))
    return pl.pallas_call(
        _nsa_cmp_kernel,
        grid=(B,),
        in_specs=[pl.BlockSpec((1, S, LANES), lambda b: (b, 0, 0)), full((1, F)), full((1, F)),
                  full((F, 2 * CMP_HIDDEN)), full((F, 2 * CMP_HIDDEN)), full((1, 2 * CMP_HIDDEN)),
                  full((CMP_HIDDEN, LANES)), full((CMP_HIDDEN, LANES))],
        out_specs=[pl.BlockSpec((1, n, LANES), lambda b: (b, 0, 0)), pl.BlockSpec((1, LANES, n), lambda b: (b, 0, 0))],
        out_shape=[jax.ShapeDtypeStruct((B, n, LANES), BF16), jax.ShapeDtypeStruct((B, LANES, n), BF16)],
        scratch_shapes=[pltpu.VMEM((n + 8, 2 * CMP_HIDDEN), F32)],
        compiler_params=_cparams(("parallel",)),
        name="nsa_compress",
    )(kcvc, ptop, pbot, wtop, wbot, b1, w2k, w2v)


def _nsa_attn_kernel(qt_ref, g_ref, kc_ref, vct_ref, ovl_ref, ksa_ref, vst_ref, kw_ref, vwt_ref, o_ref, *scratch):
    H, TQ, TK = NSA_HEADS, NSA_TQ, NSA_TK
    M = H * TQ
    NB = MAX_SEL_BLOCKS
    s0 = pl.program_id(1) * TQ
    ncmp = kc_ref.shape[1]
    qt = qt_ref[0, 0]
    t_row = s0 + lax.broadcasted_iota(jnp.int32, (1, M), 1) % TQ

    cend = lax.broadcasted_iota(jnp.int32, (ncmp, M), 0) * CMP_STRIDE + (CMP_LEN - 1)
    s_c = jnp.where(cend <= t_row, _dot(kc_ref[0], qt), NEG_INF)
    m_c = jnp.max(s_c, axis=0, keepdims=True)
    e_c = jnp.exp2(s_c - m_c)
    a_c = _dot(vct_ref[0], e_c.astype(BF16))
    inv_c = jnp.where(m_c > 0.5 * NEG_INF, 1.0 / jnp.maximum(a_c[V_ONES:V_ONES + 1, :], 1e-30), 0.0)
    o_c = a_c * inv_c
    p_c = e_c * inv_c

    WK = WINDOW + TQ
    w0 = pl.multiple_of(jnp.maximum(s0 - WINDOW, 0), TQ)
    dist = t_row - (w0 + lax.broadcasted_iota(jnp.int32, (WK, M), 0))
    in_window = lax.bitcast_convert_type(dist, jnp.uint32) < jnp.uint32(WINDOW)
    s_w = jnp.where(in_window, _dot(kw_ref[0, pl.ds(w0, WK), :], qt), NEG_INF)
    e_w = jnp.exp2(s_w - jnp.max(s_w, axis=0, keepdims=True)).astype(BF16)
    a_w = _dot(vwt_ref[0, w0 // LANES], e_w[0:LANES])
    for c in range(1, WK // LANES):
        a_w = a_w + _dot(vwt_ref[0, w0 // LANES + c], e_w[c * LANES:(c + 1) * LANES])
    o_w = _normalize(a_w)

    psum = p_c[:, 0:TQ] + p_c[:, TQ:2 * TQ] + p_c[:, 2 * TQ:3 * TQ] + p_c[:, 3 * TQ:4 * TQ]
    p_hi = psum.astype(BF16)
    p_lo = (psum - p_hi.astype(F32)).astype(BF16)
    imp = _dot(ovl_ref[...], p_hi) + _dot(ovl_ref[...], p_lo)
    jb = lax.broadcasted_iota(jnp.int32, (NB, TQ), 0)
    tq_l = s0 + lax.broadcasted_iota(jnp.int32, (NB, TQ), 1)
    cur = lax.shift_right_logical(tq_l, 6)
    forced = (jb == 0) | (jb == cur) | (jb == cur - 1)
    score = jnp.where(forced, -3e38, jnp.where(jb * SEL_LEN <= tq_l, imp, NEG_INF))
    jbf = jb.astype(F32)
    sel = jnp.where(forced, 1.0, 0.0)
    for _ in range(SEL_TOPN - 3):
        best = jnp.max(score, axis=0, keepdims=True)
        first = jnp.min(jnp.where(score == best, jbf, float(NB)), axis=0, keepdims=True)
        hit = jbf == first
        sel = jnp.where(hit, 1.0, sel)
        score = jnp.where(hit, -3e38, score)
    bias = ((sel - 1.0) * (-NEG_INF)).astype(BF16)
    q_aug = jnp.concatenate([qt, jnp.concatenate([bias] * H, axis=1)], axis=0)

    NC = NSA_CHAINS
    CW = M // NC
    m_refs, acc_refs, buf_a, buf_b = (scratch[i * NC:(i + 1) * NC] for i in range(4))
    n_full = s0 // TK
    q_cols = [q_aug[:, c * CW:(c + 1) * CW] for c in range(NC)]

    def scores(j):
        k = ksa_ref[0, pl.ds(pl.multiple_of(j * TK, TK), TK), :]
        return [_dot(k, qc) for qc in q_cols]

    def values(j):
        return [vst_ref[0, j]] * NC

    kpos = n_full * TK + lax.broadcasted_iota(jnp.int32, (TK, CW), 0)
    _flash_loop(n_full, scores, values, kpos <= t_row[:, 0:CW], buf_a, buf_b, m_refs, acc_refs)
    o_s = jnp.concatenate([_normalize(acc_refs[c][...]) for c in range(NC)], axis=1)

    g = jax.nn.sigmoid(g_ref[0]).T
    outs = []
    for h in range(H):
        cols = slice(h * TQ, (h + 1) * TQ)
        r0 = NSA_HEAD_DIM + 3 * h
        o_h = (g[r0:r0 + 1, :] * o_c[0:NSA_HEAD_DIM, cols] + g[r0 + 1:r0 + 2, :] * o_s[0:NSA_HEAD_DIM, cols]
               + g[r0 + 2:r0 + 3, :] * o_w[0:NSA_HEAD_DIM, cols])
        outs.append(o_h)
    o_ref[0] = jnp.concatenate(outs, axis=0).T


def _nsa_attn(qt, pb, kc, vct, ovl, ksa, vst, kw, vwt):
    B, S, _ = ksa.shape
    H, TQ, TK = NSA_HEADS, NSA_TQ, NSA_TK
    ncmp = kc.shape[1]
    per_b = lambda *shape: pl.BlockSpec((1,) + shape, lambda b, i: (b,) + (0,) * len(shape))
    return pl.pallas_call(
        _nsa_attn_kernel,
        grid=(B, S // TQ),
        in_specs=[pl.BlockSpec((1, 1, LANES, H * TQ), lambda b, i: (b, i, 0, 0)),
                  pl.BlockSpec((1, TQ, LANES), lambda b, i: (b, i, WB // LANES - 1)),
                  per_b(ncmp, LANES), per_b(LANES, ncmp),
                  pl.BlockSpec((MAX_SEL_BLOCKS, ncmp), lambda b, i: (0, 0)),
                  per_b(S, 2 * LANES), per_b(S // TK, LANES, TK), per_b(S, LANES), per_b(S // LANES, LANES, LANES)],
        out_specs=pl.BlockSpec((1, TQ, H * NSA_HEAD_DIM), lambda b, i: (b, i, 0)),
        out_shape=jax.ShapeDtypeStruct((B, S, H * NSA_HEAD_DIM), F32),
        scratch_shapes=([pltpu.VMEM((1, H * TQ // NSA_CHAINS), F32)] * NSA_CHAINS
                        + [pltpu.VMEM((LANES, H * TQ // NSA_CHAINS), F32)] * NSA_CHAINS
                        + [pltpu.VMEM((TK, H * TQ // NSA_CHAINS), F32)] * (2 * NSA_CHAINS)),
        compiler_params=_cparams(("parallel", "arbitrary")),
        name="nsa_attn",
    )(qt, pb, kc, vct, ovl, ksa, vst, kw, vwt)


def _nsa_overlap(ncmp):
    n = np.arange(ncmp)[None, :] * CMP_STRIDE
    s = np.arange(MAX_SEL_BLOCKS)[:, None] * SEL_LEN
    real = np.arange(ncmp)[None, :] < ncmp - 1
    ovl = (n < s + SEL_LEN) & (n + CMP_LEN > s) & real
    return jnp.asarray(ovl.astype(np.float32), dtype=BF16)


def _mix_ffn_kernel(ya_ref, yb_ref, yc_ref, yd_ref, x_ref, gn_ref, wo_ref, g_ref, wgu_ref, wd_ref, fn_ref, o_ref, *,
                    final_norm):
    x = x_ref[...]
    for g, y_ref in enumerate((ya_ref, yb_ref, yc_ref, yd_ref)):
        sl = slice(g * GROUP_WIDTH, (g + 1) * GROUP_WIDTH)
        y = (_rms(y_ref[...], GROUP_WIDTH) * gn_ref[:, sl]).astype(BF16)
        x = x + _dot(y, wo_ref[sl, :])
    o_ref[...] = x
    x = o_ref[...]
    h = (_rms(x, D_MODEL) * g_ref[...]).astype(BF16)
    acc = x
    for c in range(D_FF // FF_CHUNK):
        lo = c * FF_CHUNK
        gate = _dot(h, wgu_ref[:, lo:lo + FF_CHUNK])
        up = _dot(h, wgu_ref[:, D_FF + lo:D_FF + lo + FF_CHUNK])
        acc = acc + _dot((jax.nn.silu(gate) * up).astype(BF16), wd_ref[lo:lo + FF_CHUNK, :])
    if final_norm:
        acc = _rms(acc, D_MODEL) * fn_ref[...]
    o_ref[...] = acc


def _mix_ffn(ys, x2, gn, w_out_b, gain, wgu_b, wd_b, fnorm, layer, final_norm, tm=512):
    T = x2.shape[0]
    W = GROUP_WIDTH
    row = pl.BlockSpec((1, D_MODEL), lambda i: (0, 0))
    resident = lambda shape: pl.BlockSpec((None,) + shape, lambda i: (layer, 0, 0))
    return pl.pallas_call(
        functools.partial(_mix_ffn_kernel, final_norm=final_norm),
        grid=(T // tm,),
        in_specs=[pl.BlockSpec((tm, W), lambda i: (i, 0))] * 4
        + [pl.BlockSpec((tm, D_MODEL), lambda i: (i, 0)), row, resident((D_MODEL, D_MODEL)), row,
           resident((D_MODEL, 2 * D_FF)), resident((D_FF, D_MODEL)), row],
        out_specs=pl.BlockSpec((tm, D_MODEL), lambda i: (i, 0)),
        out_shape=jax.ShapeDtypeStruct((T, D_MODEL), F32),
        compiler_params=_cparams(("parallel",)),
        name="mix_ffn",
    )(*ys, x2, gn.reshape(1, D_MODEL), w_out_b, gain.reshape(1, D_MODEL), wgu_b, wd_b, fnorm.reshape(1, D_MODEL))


def _gather_cols(w, layout):
    parts = []
    for name, width in layout:
        if name is None:
            parts.append(jnp.zeros(w.shape[:-1] + (width,), w.dtype))
        else:
            o, n = _OFF[name]
            assert n == width
            parts.append(w[..., o:o + n])
    return jnp.concatenate(parts, axis=-1)


def _block_diag(w):
    h, d, _ = w.shape
    out = jnp.zeros((h * d, h * d), w.dtype)
    for i in range(h):
        out = out.at[i * d:(i + 1) * d, i * d:(i + 1) * d].set(w[i])
    return out


def _pad_to(a, shape):
    return jnp.pad(a, [(0, s - d) for s, d in zip(shape, a.shape)])


def _mla_weights(w_uq, w_ukv):
    H = MLA_HEADS
    wq = w_uq.reshape(MLA_Q_RANK, H, MLA_QK_DIM)
    wq = _pad_to(wq, (256, H, LANES)).reshape(256, H * LANES)
    wkv = w_ukv.reshape(MLA_KV_RANK, H, MLA_NOPE_DIM + MLA_V_DIM)
    wk = _pad_to(wkv[:, :, :MLA_NOPE_DIM], (MLA_KV_RANK, H, LANES)).reshape(MLA_KV_RANK, H * LANES)
    wv = _pad_to(wkv[:, :, MLA_NOPE_DIM:], (MLA_KV_RANK, H, LANES)).reshape(MLA_KV_RANK, H * LANES)
    return wq.astype(BF16), jnp.concatenate([wk, wv], axis=1).astype(BF16)


def _cmp_weights(cmp_pos, cmp_w1, cmp_b1, cmp_w2):
    half = CMP_LEN // 2
    Dh = NSA_HEAD_DIM
    w1 = cmp_w1.reshape(2, CMP_LEN, Dh, CMP_HIDDEN)

    def rows(part):
        wk = w1[0, part * half:(part + 1) * half]
        wv = w1[1, part * half:(part + 1) * half]
        z = jnp.zeros_like(wk)
        top = jnp.concatenate([wk, z], axis=-1)
        bot = jnp.concatenate([z, wv], axis=-1)
        return jnp.concatenate([top, bot], axis=1).reshape(half * 2 * Dh, 2 * CMP_HIDDEN)

    def pos_row(part):
        p = jnp.concatenate([cmp_pos[0, part * half:(part + 1) * half], cmp_pos[1, part * half:(part + 1) * half]],
                            axis=-1)
        return p.reshape(1, half * 2 * Dh)

    b1 = jnp.concatenate([cmp_b1[0], cmp_b1[1]]).reshape(1, 2 * CMP_HIDDEN)
    w2k = _pad_to(cmp_w2[0], (CMP_HIDDEN, LANES)).astype(BF16)
    w2v = _pad_to(cmp_w2[1], (CMP_HIDDEN, LANES)).astype(BF16)
    return pos_row(0), pos_row(1), rows(0).astype(BF16), rows(1).astype(BF16), b1, w2k, w2v


def kernel(x, positions, norm_mix, w_in, conv_w, conv_b, lru_wa, lru_ba, lru_wx, lru_bx, lru_lambda, cmp_pos, cmp_w1, cmp_b1, cmp_w2, gla_wg2, gla_bg2, gla_norm, mla_q_norm, mla_kv_norm, mla_w_uq, mla_w_ukv, group_norm, w_out, norm_ffn, w_gate_up, w_down, final_norm):
    B, S, D = x.shape
    assert D == D_MODEL and S % (CMP_STRIDE * 8) == 0 and S // SEL_LEN <= MAX_SEL_BLOCKS and S >= WINDOW + NSA_TQ
    depth = w_in.shape[0]
    T = B * S
    pos3 = positions.reshape(B, S, 1)
    ncmp = S // CMP_STRIDE
    ovl = _nsa_overlap(ncmp)
    cos_n, sin_n, cos_m, sin_m = _rope_tables(pos3)
    x2 = x.reshape(T, D)
    w_out_b, wgu_b, wd_b = w_out.astype(BF16), w_gate_up.astype(BF16), w_down.astype(BF16)
    w_p = jnp.concatenate([_gather_cols(w_in.astype(BF16), lay) for lay in (_LAYOUT_A, _LAYOUT_B, _LAYOUT_C, _LAYOUT_D)],
                          axis=-1)
    for l in range(depth):
        pa, pb, pc, pd = _inproj(x2, norm_mix[l], w_p, l)
        pa, pb, pc, pd = (p.reshape(B, S, -1) for p in (pa, pb, pc, pd))

        wg = jnp.concatenate([_block_diag(lru_wa[l]), _block_diag(lru_wx[l])], axis=1).astype(BF16)
        bg = jnp.concatenate([lru_ba[l], lru_bx[l]]).reshape(1, 2 * GROUP_WIDTH)
        y_a = _rglru(pa, conv_w[l], conv_b[l], wg, bg, lru_lambda[l])

        qt_n, ksa, vst, kw, vwt, kcvc = _nsa_proj(pb, cos_n, sin_n)
        kc, vct = _nsa_compress(kcvc, *_cmp_weights(cmp_pos[l], cmp_w1[l], cmp_b1[l], cmp_w2[l]))
        y_b = _nsa_attn(qt_n, pb, kc, vct, ovl, ksa, vst, kw, vwt)

        wg2_p = _pad_to(gla_wg2[l], (LANES, GLA_HEADS * GLA_DK))
        y_c = _gla(pc, wg2_p, gla_bg2[l], jnp.tile(gla_norm[l], GLA_HEADS))

        wq_p, wkv_p = _mla_weights(mla_w_uq[l], mla_w_ukv[l])
        qt_m, k_m, vt_m = _mla_proj(pd, cos_m, sin_m, _pad_to(mla_q_norm[l], (256,)).reshape(1, 256), mla_kv_norm[l],
                                    wq_p, wkv_p)
        y_d = _mla_attn(qt_m, k_m, vt_m)

        ys = [y.reshape(T, GROUP_WIDTH) for y in (y_a, y_b, y_c, y_d)]
        x2 = _mix_ffn(ys, x2, group_norm[l], w_out_b, norm_ffn[l], wgu_b, wd_b, final_norm, l,
                      final_norm=(l == depth - 1))
    return x2.reshape(B, S, D)
```
